```python
import jax, jax.numpy as jnp
from jax import lax
import numpy as np

D_MODEL = 1024
BATCH = 16
SEQ = 4096
DEPTH = 1
DEC_BATCH = 4
DEC_SEQ = 8192
PAST_LEN = 128

D_RWKV = D_MODEL // 2
HEAD_DIM = 64
N_HEADS = D_RWKV // HEAD_DIM
D_CONV = D_MODEL - D_RWKV
D_MIX = D_RWKV + D_CONV
DECAY_RANK = 64
AAA_RANK = 64
GATE_RANK = 128
CONV_WIDTH = 31
N_EXPERTS = 32
TOP_K = 4
D_FF = D_MODEL
SWIGLU_LIMIT = 7.0
SWIGLU_ALPHA = 1.702
LN_EPS = 1e-5
GN_EPS = 64e-5
DEEPNORM_ALPHA = (2.0 * DEPTH) ** 0.25
DEEPNORM_BETA = (8.0 * DEPTH) ** -0.25

K_OFF = D_RWKV
V_OFF = 2 * D_RWKV
WL_OFF = 3 * D_RWKV
AL_OFF = WL_OFF + 2 * DECAY_RANK
GL_OFF = AL_OFF + 2 * AAA_RANK
N_SHIFT_COLS = GL_OFF + GATE_RANK
D_IN = N_SHIFT_COLS + 2 * D_CONV

kernel_name = 'hybrid_rwkv7_conformer_moe_encoder'


def layer_norm(x, g, b, eps=LN_EPS):
    xf = x.astype(jnp.float32)
    mu = jnp.mean(xf, -1, keepdims=True)
    var = jnp.mean(jnp.square(xf - mu), -1, keepdims=True)
    return ((xf - mu) * lax.rsqrt(var + eps) * g + b).astype(x.dtype)


def centred_shift(p, mu):
    prev = jnp.pad(p[:, :-1], ((0, 0), (1, 0), (0, 0)))
    nxt = jnp.pad(p[:, 1:], ((0, 0), (0, 1), (0, 0)))
    return p + mu[0] * (prev - p) + mu[1] * (nxt - p)


def wkv_scan(r, w, k, v, z, b, reverse):
    Bsz, T, H, N = r.shape
    xs = tuple(jnp.moveaxis(t, 1, 0) for t in (r, w, k, v, z, b))

    def step(S, inp):
        r_t, w_t, k_t, v_t, z_t, b_t = inp
        sa = jnp.einsum('bhvk,bhk->bhv', S, z_t)
        S = S * w_t[:, :, None, :] + sa[..., None] * b_t[:, :, None, :] + v_t[..., None] * k_t[:, :, None, :]
        return S, jnp.einsum('bhvk,bhk->bhv', S, r_t)

    S0 = jnp.zeros((Bsz, H, N, N), jnp.float32)
    _, o = lax.scan(step, S0, xs, reverse=reverse)
    return jnp.moveaxis(o, 0, 1)


def rwkv7_group(p, mu_shift, w0, w2, a0, a2, g_up, k_k, k_a, r_k, gn_g, gn_b):
    dt = p.dtype
    Bsz, T, _ = p.shape
    p = centred_shift(p, mu_shift).astype(jnp.float32)
    r, k, v, wl, al, gl = jnp.split(p, [K_OFF, V_OFF, WL_OFF, AL_OFF, GL_OFF], axis=-1)
    wl = wl.reshape(Bsz, T, 2, DECAY_RANK)
    al = al.reshape(Bsz, T, 2, AAA_RANK)
    w_log = -jax.nn.softplus(-(w0 + jnp.einsum('btdr,drc->btdc', jnp.tanh(wl), w2))) - 0.5
    decay = jnp.exp(-jnp.exp(w_log))
    a = jax.nn.sigmoid(a0 + jnp.einsum('btdr,drc->btdc', al, a2))
    g = jax.nn.sigmoid(gl) @ g_up

    def heads(t):
        return t.reshape(t.shape[:-1] + (N_HEADS, HEAD_DIM))

    r_h, v_h = heads(r), heads(v)
    kk = heads(k * k_k)
    kk = kk * lax.rsqrt(jnp.maximum(jnp.sum(kk * kk, -1, keepdims=True), 1e-24))
    wkv_dirs, bonus_dirs = [], []
    for d in range(2):
        a_d = a[:, :, d]
        k_d = heads(k * (1.0 + (a_d - 1.0) * k_a))
        wkv_dirs.append(wkv_scan(r_h, heads(decay[:, :, d]), k_d, v_h, -kk, kk * heads(a_d), reverse=(d == 1)))
        bonus_dirs.append(jnp.sum(r_h * k_d * r_k, -1, keepdims=True) * v_h)
    wkv = wkv_dirs[0] + wkv_dirs[1]
    mu = jnp.mean(wkv, -1, keepdims=True)
    var = jnp.mean(jnp.square(wkv - mu), -1, keepdims=True)
    o = (wkv - mu) * lax.rsqrt(var + GN_EPS) * heads(gn_g) + heads(gn_b)
    o = (o + bonus_dirs[0] + bonus_dirs[1]).reshape(Bsz, T, D_RWKV) * g
    return o.astype(dt)


def conformer_conv_group(p, dw_w, dw_b, cln_g, cln_b):
    u, gate = jnp.split(p, 2, axis=-1)
    h = u * jax.nn.sigmoid(gate)
    h = lax.conv_general_dilated(h, dw_w[:, None, :], (1,), [(CONV_WIDTH // 2, CONV_WIDTH // 2)],
                                 dimension_numbers=('NWC', 'WIO', 'NWC'), feature_group_count=D_CONV) + dw_b
    return jax.nn.silu(layer_norm(h, cln_g, cln_b))


def moe(x, w_router, b_router, w_gu, b_gu, w_down, b_down):
    Bsz, T, D = x.shape
    xt = x.reshape(-1, D)
    logits = (xt @ w_router + b_router).astype(jnp.float32)
    top_v, top_i = lax.top_k(logits, TOP_K)
    top_w = jax.nn.softmax(top_v, axis=-1)
    gates = jnp.einsum('tk,tke->te', top_w, jax.nn.one_hot(top_i, N_EXPERTS, dtype=jnp.float32)).astype(x.dtype)
    out = jnp.zeros_like(xt)
    for e in range(N_EXPERTS):
        h = xt @ w_gu[e] + b_gu[e]
        h_glu, h_lin = jnp.split(h, 2, axis=-1)
        h_glu = jnp.minimum(h_glu, SWIGLU_LIMIT)
        h_lin = jnp.clip(h_lin, -SWIGLU_LIMIT, SWIGLU_LIMIT)
        y = (h_lin + 1.0) * (h_glu * jax.nn.sigmoid(SWIGLU_ALPHA * h_glu))
        out = out + gates[:, e:e + 1] * (y @ w_down[e] + b_down[e])
    return out.reshape(Bsz, T, D)


def encoder_layer(x, c, w_mod, b_mod, w_in, mu_shift, w0, w2, a0, a2, g_up, k_k, k_a, r_k, gn_g, gn_b,
                  dw_w, dw_b, cln_g, cln_b, w_out, ln1_g, ln1_b, w_router, b_router, w_gu, b_gu,
                  w_down, b_down, ln2_g, ln2_b):
    mod = (jax.nn.silu(c) @ w_mod + b_mod)[:, None, :]
    sh1, sc1, gt1, sh2, sc2, gt2 = jnp.split(mod, 6, axis=-1)
    h = x * (1.0 + sc1) + sh1
    p = h @ w_in
    y_a = rwkv7_group(p[..., :N_SHIFT_COLS], mu_shift, w0, w2, a0, a2, g_up, k_k, k_a, r_k, gn_g, gn_b)
    y_b = conformer_conv_group(p[..., N_SHIFT_COLS:], dw_w, dw_b, cln_g, cln_b)
    mix = jnp.concatenate([y_a, y_b], axis=-1) @ w_out
    x = layer_norm(DEEPNORM_ALPHA * x + (1.0 + gt1) * mix, ln1_g, ln1_b)
    h = x * (1.0 + sc2) + sh2
    f = moe(h, w_router, b_router, w_gu, b_gu, w_down, b_down)
    return layer_norm(DEEPNORM_ALPHA * x + (1.0 + gt2) * f, ln2_g, ln2_b)


def encoder(x, c, in_g, in_b, layer_params):
    x = layer_norm(x, in_g, in_b)
    for l in range(DEPTH):
        x = encoder_layer(x, c, *[prm[l] for prm in layer_params])
    return x


def setup_inputs(seed: int = 0) -> dict:
    key = jax.random.key(seed)
    ks = iter(jax.random.split(key, 40))
    L = DEPTH

    def nrm(shape, s):
        return s * jax.random.normal(next(ks), shape, jnp.float32)

    return {
        'x_prompt': nrm((BATCH, SEQ, D_MODEL), 1.0),
        'x_sample': nrm((DEC_BATCH, DEC_SEQ, D_MODEL), 1.0),
        'c_prompt': nrm((BATCH, D_MODEL), 1.0),
        'c_sample': nrm((DEC_BATCH, D_MODEL), 1.0),
        'in_g': 1.0 + nrm((D_MODEL,), 0.02),
        'in_b': nrm((D_MODEL,), 0.02),
        'w_mod': nrm((L, D_MODEL, 6 * D_MODEL), 0.1 * D_MODEL ** -0.5),
        'b_mod': nrm((L, 6 * D_MODEL), 0.01),
        'w_in': nrm((L, D_MODEL, D_IN), D_MODEL ** -0.5),
        'mu_shift': 0.25 + nrm((L, 2, N_SHIFT_COLS), 0.1),
        'w0': jax.random.uniform(next(ks), (L, 2, D_RWKV), jnp.float32, -5.5, 0.5),
        'w2': nrm((L, 2, DECAY_RANK, D_RWKV), 0.1 * DECAY_RANK ** -0.5),
        'a0': nrm((L, 2, D_RWKV), 0.1),
        'a2': nrm((L, 2, AAA_RANK, D_RWKV), 0.1 * AAA_RANK ** -0.5),
        'g_up': nrm((L, GATE_RANK, D_RWKV), GATE_RANK ** -0.5),
        'k_k': 0.85 + nrm((L, D_RWKV), 0.05),
        'k_a': 1.0 + nrm((L, D_RWKV), 0.05),
        'r_k': nrm((L, N_HEADS, HEAD_DIM), 0.1),
        'gn_g': 1.0 + nrm((L, D_RWKV), 0.02),
        'gn_b': nrm((L, D_RWKV), 0.02),
        'dw_w': nrm((L, CONV_WIDTH, D_CONV), CONV_WIDTH ** -0.5),
        'dw_b': nrm((L, D_CONV), 0.02),
        'cln_g': 1.0 + nrm((L, D_CONV), 0.02),
        'cln_b': nrm((L, D_CONV), 0.02),
        'w_out': nrm((L, D_MIX, D_MODEL), DEEPNORM_BETA * D_MIX ** -0.5),
        'ln1_g': 1.0 + nrm((L, D_MODEL), 0.02),
        'ln1_b': nrm((L, D_MODEL), 0.02),
        'w_router': nrm((L, D_MODEL, N_EXPERTS), D_MODEL ** -0.5),
        'b_router': nrm((L, N_EXPERTS), 0.01),
        'w_gu': nrm((L, N_EXPERTS, D_MODEL, 2 * D_FF), D_MODEL ** -0.5),
        'b_gu': nrm((L, N_EXPERTS, 2 * D_FF), 0.01),
        'w_down': nrm((L, N_EXPERTS, D_FF, D_MODEL), DEEPNORM_BETA * D_FF ** -0.5),
        'b_down': nrm((L, N_EXPERTS, D_MODEL), 0.01),
        'ln2_g': 1.0 + nrm((L, D_MODEL), 0.02),
        'ln2_b': nrm((L, D_MODEL), 0.02),
    }


def reference(x_prompt, x_sample, c_prompt, c_sample, in_g, in_b, w_mod, b_mod, w_in, mu_shift, w0, w2,
              a0, a2, g_up, k_k, k_a, r_k, gn_g, gn_b, dw_w, dw_b, cln_g, cln_b, w_out, ln1_g, ln1_b,
              w_router, b_router, w_gu, b_gu, w_down, b_down, ln2_g, ln2_b):
    layer_params = (w_mod, b_mod, w_in, mu_shift, w0, w2, a0, a2, g_up, k_k, k_a, r_k, gn_g, gn_b,
                    dw_w, dw_b, cln_g, cln_b, w_out, ln1_g, ln1_b, w_router, b_router, w_gu, b_gu,
                    w_down, b_down, ln2_g, ln2_b)
    y_prompt = encoder(x_prompt, c_prompt, in_g, in_b, layer_params)
    y_sample = encoder(x_sample, c_sample, in_g, in_b, layer_params)
    return (y_prompt, y_sample)
```

```python
import functools

import jax
import jax.numpy as jnp
from jax import lax
from jax.experimental import pallas as pl
from jax.experimental.pallas import tpu as pltpu

F32 = jnp.float32
BF16 = jnp.bfloat16

D_MODEL = 1024
D_RWKV = 512
HEAD_DIM = 64
N_HEADS = 8
D_CONV = 512
DECAY_RANK = 64
AAA_RANK = 64
GATE_RANK = 128
CONV_WIDTH = 31
N_EXPERTS = 32
TOP_K = 4
D_FF = 1024
SWIGLU_LIMIT = 7.0
SWIGLU_ALPHA = 1.702
LN_EPS = 1e-5
GN_EPS = 64e-5
DEPTH = 1
DEEPNORM_ALPHA = (2.0 * DEPTH) ** 0.25
N_SHIFT_COLS = 3 * D_RWKV + 2 * DECAY_RANK + 2 * AAA_RANK + GATE_RANK

LANES = 128
HALO = 16
CHUNK = 64
GROUP = 4
GROUP_W = GROUP * HEAD_DIM
VMEM_LIMIT = 56 * 1024 * 1024


def _params(sem):
    return pltpu.CompilerParams(dimension_semantics=sem, vmem_limit_bytes=VMEM_LIMIT)


def _sigmoid(x):
    return 1.0 / (1.0 + jnp.exp(-x))


def _mm(a, b):
    return jnp.dot(a.astype(BF16), b.astype(BF16), preferred_element_type=F32)


def _mm_nt(a, b):
    return lax.dot_general(a.astype(BF16), b.astype(BF16), (((1,), (1,)), ((), ())),
                           preferred_element_type=F32)


def _mm_tn(a, b):
    return lax.dot_general(a.astype(BF16), b.astype(BF16), (((0,), (0,)), ((), ())),
                           preferred_element_type=F32)


def _split(a, n):
    parts = []
    for _ in range(n):
        h = a.astype(BF16)
        parts.append(h)
        a = a - h.astype(F32)
    return parts


def _mm_exact_rhs(a, b_bf16, n):
    acc = None
    for h in _split(a, n):
        t = jnp.dot(h, b_bf16, preferred_element_type=F32)
        acc = t if acc is None else acc + t
    return acc


def _mm_exact_lhs(a_bf16, b, n):
    acc = None
    for h in _split(b, n):
        t = jnp.dot(a_bf16, h, preferred_element_type=F32)
        acc = t if acc is None else acc + t
    return acc


def _mm3(a, b):
    ah, al = _split(a, 2)
    bh, bl = _split(b, 2)
    return (jnp.dot(ah, bh, preferred_element_type=F32) + jnp.dot(al, bh, preferred_element_type=F32)
            + jnp.dot(ah, bl, preferred_element_type=F32))


def _ln(x, g, b, eps):
    mu = jnp.mean(x, -1, keepdims=True)
    xc = x - mu
    var = jnp.mean(xc * xc, -1, keepdims=True)
    return xc * lax.rsqrt(var + eps) * g + b


def _row(a):
    return a.reshape(1, -1)


def _mod_kernel(c_ref, w_ref, b_ref, o_ref):
    c = c_ref[...]
    o_ref[...] = _mm3(c * _sigmoid(c), w_ref[...]) + b_ref[...]


def _mod_call(c, w_mod, b_mod):
    n, d = c.shape
    npad = -(-n // 8) * 8
    cp = jnp.pad(c, ((0, npad - n), (0, 0)))
    tn = 1536
    out = pl.pallas_call(
        _mod_kernel,
        grid=(6 * d // tn,),
        in_specs=[pl.BlockSpec((npad, d), lambda j: (0, 0)),
                  pl.BlockSpec((d, tn), lambda j: (0, j)),
                  pl.BlockSpec((1, tn), lambda j: (0, j))],
        out_specs=pl.BlockSpec((npad, tn), lambda j: (0, j)),
        out_shape=jax.ShapeDtypeStruct((npad, 6 * d), F32),
        compiler_params=_params(("arbitrary",)),
        name="mod",
    )(cp, w_mod, _row(b_mod))
    return out[:n].reshape(n, 6, d)


def _inproj_kernel(x_ref, mod_ref, g_ref, b_ref, wr_ref, wc_ref, pr_ref, pc_ref):
    x0 = _ln(x_ref[0], g_ref[...], b_ref[...], LN_EPS)
    m = mod_ref[0]
    h = (x0 * (1.0 + m[1:2]) + m[0:1]).astype(BF16)
    pr_ref[0] = jnp.dot(h, wr_ref[...], preferred_element_type=F32).astype(pr_ref.dtype)
    pc_ref[0] = jnp.dot(h, wc_ref[...], preferred_element_type=F32).astype(pc_ref.dtype)


def _inproj_call(x, mod, in_g, in_b, w_r, w_c):
    B, T, D = x.shape
    tm = min(512, T)
    const = lambda b, t: (0, 0)
    return pl.pallas_call(
        _inproj_kernel,
        grid=(B, T // tm),
        in_specs=[pl.BlockSpec((1, tm, D), lambda b, t: (b, t, 0)),
                  pl.BlockSpec((1, 6, D), lambda b, t: (b, 0, 0)),
                  pl.BlockSpec((1, D), const), pl.BlockSpec((1, D), const),
                  pl.BlockSpec(w_r.shape, const), pl.BlockSpec(w_c.shape, const)],
        out_specs=[pl.BlockSpec((1, tm, N_SHIFT_COLS), lambda b, t: (b, t, 0)),
                   pl.BlockSpec((1, tm, 2 * D_CONV), lambda b, t: (b, t, 0))],
        out_shape=[jax.ShapeDtypeStruct((B, T, N_SHIFT_COLS), BF16),
                   jax.ShapeDtypeStruct((B, T, 2 * D_CONV), BF16)],
        compiler_params=_params(("parallel", "parallel")),
        name="inproj",
    )(x, mod, _row(in_g), _row(in_b), w_r, w_c)


def _pre_kernel(p_ref, pp_ref, pn_ref, mu_ref, w0_ref, w2_ref, a0_ref, a2_ref, gup_ref, kkw_ref, ka_ref,
                rk_ref, bd_ref, r_o, v_o, kk_o, lw_o, kd_o, bb_o, g_o, bonus_o):
    t = pl.program_id(1)
    nt = pl.num_programs(1)
    p = p_ref[0].astype(F32)
    tm = p.shape[0]
    prev_row = jnp.where(t > 0, pp_ref[0].astype(F32)[HALO - 1:HALO], 0.0)
    next_row = jnp.where(t < nt - 1, pn_ref[0].astype(F32)[0:1], 0.0)
    row = lax.broadcasted_iota(jnp.int32, p.shape, 0)
    prev = jnp.where(row == 0, prev_row, pltpu.roll(p, 1, 0))
    nxt = jnp.where(row == tm - 1, next_row, pltpu.roll(p, tm - 1, 0))
    mu = mu_ref[...]
    ps = p + mu[0:1] * (prev - p) + mu[1:2] * (nxt - p)

    r = ps[:, 0:D_RWKV]
    k = ps[:, D_RWKV:2 * D_RWKV]
    v = ps[:, 2 * D_RWKV:3 * D_RWKV]
    o = 3 * D_RWKV
    wl = ps[:, o:o + 2 * DECAY_RANK]
    al = ps[:, o + 2 * DECAY_RANK:o + 2 * DECAY_RANK + 2 * AAA_RANK]
    gl = ps[:, o + 2 * DECAY_RANK + 2 * AAA_RANK:]

    y = -(w0_ref[...] + _mm(jnp.tanh(wl), w2_ref[...]))
    softplus = jnp.maximum(y, 0.0) + jnp.log(1.0 + jnp.exp(-jnp.abs(y)))
    lw = -jnp.exp(-softplus - 0.5)
    a = _sigmoid(a0_ref[...] + _mm(al, a2_ref[...]))
    g = _mm(_sigmoid(gl), gup_ref[...])

    bd = bd_ref[...]
    kk = k * kkw_ref[...]
    ss = _mm_exact_rhs(kk * kk, bd, 2)
    kk = kk * lax.rsqrt(jnp.maximum(ss, 1e-24))
    ka = ka_ref[...]
    a0d, a1d = a[:, :D_RWKV], a[:, D_RWKV:]
    k0 = k * (1.0 + (a0d - 1.0) * ka)
    k1 = k * (1.0 + (a1d - 1.0) * ka)
    bonus = _mm_exact_rhs(r * (k0 + k1) * rk_ref[...], bd, 2) * v

    r_o[0] = r
    v_o[0] = v
    kk_o[0] = kk
    lw_o[0] = lw
    kd_o[0, :, :D_RWKV] = k0
    kd_o[0, :, D_RWKV:] = k1
    bb_o[0, :, :D_RWKV] = kk * a0d
    bb_o[0, :, D_RWKV:] = kk * a1d
    g_o[0] = g
    bonus_o[0] = bonus


def _pre_call(p_r, mu_shift, w0c, w2c, a0c, a2c, g_up, k_k, k_a, r_k, bd):
    B, T, C = p_r.shape
    tm = min(256, T)
    nh = tm // HALO
    const = lambda b, t: (0, 0)
    tile = lambda w: pl.BlockSpec((1, tm, w), lambda b, t: (b, t, 0))
    outs = [D_RWKV, D_RWKV, D_RWKV, 2 * D_RWKV, 2 * D_RWKV, 2 * D_RWKV, D_RWKV, D_RWKV]
    return pl.pallas_call(
        _pre_kernel,
        grid=(B, T // tm),
        in_specs=[tile(C),
                  pl.BlockSpec((1, HALO, C), lambda b, t: (b, jnp.maximum(t * nh - 1, 0), 0)),
                  pl.BlockSpec((1, HALO, C), lambda b, t: (b, jnp.minimum((t + 1) * nh, T // HALO - 1), 0)),
                  pl.BlockSpec(mu_shift.shape, const), pl.BlockSpec(w0c.shape, const),
                  pl.BlockSpec(w2c.shape, const), pl.BlockSpec(a0c.shape, const),
                  pl.BlockSpec(a2c.shape, const), pl.BlockSpec(g_up.shape, const),
                  pl.BlockSpec(k_k.shape, const), pl.BlockSpec(k_a.shape, const),
                  pl.BlockSpec(r_k.shape, const), pl.BlockSpec(bd.shape, const)],
        out_specs=[tile(w) for w in outs],
        out_shape=[jax.ShapeDtypeStruct((B, T, w), F32) for w in outs],
        compiler_params=_params(("parallel", "parallel")),
        name="pre",
    )(p_r, p_r, p_r, mu_shift, w0c, w2c, a0c, a2c, g_up, k_k, k_a, r_k, bd)


def _group_chunk(rt, zt, kt, bt, khat, bhat, v, S, wtot, reverse):
    L = rt.shape[0]
    R = GROUP * L
    rowh = lax.broadcasted_iota(jnp.int32, (R, GROUP_W), 0) // L
    colh = lax.broadcasted_iota(jnp.int32, (R, GROUP_W), 1) // HEAD_DIM
    own = rowh == colh

    def stack(x):
        return jnp.concatenate([x] * GROUP, axis=0)

    def stack_own(x):
        return jnp.where(own, stack(x), 0.0).astype(BF16)

    ztm, rtm = stack_own(zt), stack_own(rt)
    bt4, kt4 = stack(bt).astype(BF16), stack(kt).astype(BF16)

    rr = lax.broadcasted_iota(jnp.int32, (R, R), 0)
    cc = lax.broadcasted_iota(jnp.int32, (R, R), 1)
    tl = rr % L
    if reverse:
        delta, span = cc - rr, (L - 1) - tl
    else:
        delta, span = rr - cc, tl
    strict = (delta - 1).astype(jnp.uint32) < span.astype(jnp.uint32)
    incl = delta.astype(jnp.uint32) <= span.astype(jnp.uint32)

    n_zb = jnp.where(strict, _mm_nt(ztm, bt4), 0.0)
    a_zk = jnp.where(strict, _mm_nt(ztm, kt4), 0.0)
    a_rb = jnp.where(incl, _mm_nt(rtm, bt4), 0.0)
    a_rk = jnp.where(incl, _mm_nt(rtm, kt4), 0.0)

    tinv = jnp.where(rr == cc, 1.0, 0.0) + n_zb
    pw = n_zb
    steps = L.bit_length() - 2
    for _ in range(steps):
        pw = _mm3(pw, pw)
        tinv = tinv + _mm3(tinv, pw)

    v4 = jnp.concatenate([v[:, h * HEAD_DIM:(h + 1) * HEAD_DIM] for h in range(GROUP)], axis=0)
    u = _mm3(tinv, _mm_nt(ztm, S) + _mm(a_zk, v4))
    o4 = _mm_nt(rtm, S) + _mm(a_rb, u) + _mm(a_rk, v4)
    out = jnp.concatenate([o4[h * L:(h + 1) * L] for h in range(GROUP)], axis=1)
    s_new = S * wtot + _mm_tn(u, stack_own(bhat)) + _mm_tn(v4, stack_own(khat))
    return out, s_new


def _chunk(r, v, kk, lw, kd, bb, S, reverse):
    L = r.shape[0]
    ti = lax.broadcasted_iota(jnp.int32, (L, L), 0)
    ii = lax.broadcasted_iota(jnp.int32, (L, L), 1)
    tri = jnp.where((ii >= ti) if reverse else (ii <= ti), 1.0, 0.0).astype(BF16)
    cum = _mm_exact_lhs(tri, lw, 3)
    tot = cum[0:1] if reverse else cum[L - 1:L]
    ec, eci, ecx, eto = jnp.exp(cum), jnp.exp(-cum), jnp.exp(cum - lw), jnp.exp(tot - cum)
    rt, zt, kt, bt = r * ec, -(kk * ecx), kd * eci, bb * eci
    khat, bhat = kd * eto, bb * eto
    wtot = jnp.exp(tot)
    outs, states = [], []
    for g in range(D_RWKV // GROUP_W):
        sl = slice(g * GROUP_W, (g + 1) * GROUP_W)
        o, s = _group_chunk(rt[:, sl], zt[:, sl], kt[:, sl], bt[:, sl], khat[:, sl], bhat[:, sl],
                            v[:, sl], S[:, sl], wtot[:, sl], reverse)
        outs.append(o)
        states.append(s)
    return jnp.concatenate(outs, axis=1), jnp.concatenate(states, axis=1)


def _scan_kernel(rf, vf, kkf, lwf, kdf, bbf, rb, vb, kkb, lwb, kdb, bbb, of, ob, s_ref):
    @pl.when(pl.program_id(1) == 0)
    def _():
        s_ref[...] = jnp.zeros_like(s_ref)

    o, s = _chunk(rf[0], vf[0], kkf[0], lwf[0], kdf[0], bbf[0], s_ref[0], False)
    of[0] = o
    s_ref[0] = s
    o, s = _chunk(rb[0], vb[0], kkb[0], lwb[0], kdb[0], bbb[0], s_ref[1], True)
    ob[0] = o
    s_ref[1] = s


def _scan_call(r, v, kk, lw, kd, bb):
    B, T, C = r.shape
    L = min(CHUNK, T)
    nc = T // L
    fwd = lambda j: pl.BlockSpec((1, L, C), lambda b, c: (b, c, j))
    bwd = lambda j: pl.BlockSpec((1, L, C), lambda b, c: (b, nc - 1 - c, j))
    return pl.pallas_call(
        _scan_kernel,
        grid=(B, nc),
        in_specs=[fwd(0)] * 6 + [bwd(0)] * 3 + [bwd(1)] * 3,
        out_specs=[fwd(0), bwd(0)],
        out_shape=[jax.ShapeDtypeStruct((B, T, C), F32)] * 2,
        scratch_shapes=[pltpu.VMEM((2, HEAD_DIM, C), F32)],
        compiler_params=_params(("parallel", "arbitrary")),
        name="scan",
    )(r, v, kk, lw, kd, bb, r, v, kk, lw, kd, bb)


def _conv_kernel(p_ref, pp_ref, pn_ref, w_ref, b_ref, g_ref, be_ref, o_ref, h_ref):
    t = pl.program_id(1)
    nt = pl.num_programs(1)
    tm = p_ref.shape[1]

    def glu(x):
        x = x.astype(F32)
        return x[:, :D_CONV] * _sigmoid(x[:, D_CONV:])

    h_ref[0:HALO] = jnp.where(t > 0, glu(pp_ref[0]), 0.0)
    h_ref[HALO:HALO + tm] = glu(p_ref[0])
    h_ref[HALO + tm:] = jnp.where(t < nt - 1, glu(pn_ref[0]), 0.0)
    w = w_ref[...]
    acc = jnp.zeros((tm, D_CONV), F32) + b_ref[...]
    base = HALO - CONV_WIDTH // 2
    for j in range(CONV_WIDTH):
        acc = acc + w[j:j + 1] * h_ref[base + j:base + j + tm]
    y = _ln(acc, g_ref[...], be_ref[...], LN_EPS)
    o_ref[0] = (y * _sigmoid(y)).astype(o_ref.dtype)


def _conv_call(p_c, dw_w, dw_b, cln_g, cln_b):
    B, T, C = p_c.shape
    tm = min(256, T)
    nh = tm // HALO
    const = lambda b, t: (0, 0)
    return pl.pallas_call(
        _conv_kernel,
        grid=(B, T // tm),
        in_specs=[pl.BlockSpec((1, tm, C), lambda b, t: (b, t, 0)),
                  pl.BlockSpec((1, HALO, C), lambda b, t: (b, jnp.maximum(t * nh - 1, 0), 0)),
                  pl.BlockSpec((1, HALO, C), lambda b, t: (b, jnp.minimum((t + 1) * nh, T // HALO - 1), 0)),
                  pl.BlockSpec(dw_w.shape, const), pl.BlockSpec((1, D_CONV), const),
                  pl.BlockSpec((1, D_CONV), const), pl.BlockSpec((1, D_CONV), const)],
        out_specs=pl.BlockSpec((1, tm, D_CONV), lambda b, t: (b, t, 0)),
        out_shape=jax.ShapeDtypeStruct((B, T, D_CONV), BF16),
        scratch_shapes=[pltpu.VMEM((tm + 2 * HALO, D_CONV), F32)],
        compiler_params=_params(("parallel", "parallel")),
        name="conv",
    )(p_c, p_c, p_c, dw_w, _row(dw_b), _row(cln_g), _row(cln_b))


def _post_kernel(x_ref, mod_ref, ing_ref, inb_ref, wf_ref, wb_ref, bonus_ref, g_ref, yb_ref, gng_ref, gnb_ref,
                 bd_ref, wo_ref, l1g_ref, l1b_ref, wr_ref, br_ref, x1_o, h2_o, gate_o):
    m = mod_ref[0]
    x0 = _ln(x_ref[0], ing_ref[...], inb_ref[...], LN_EPS)
    bd = bd_ref[...]
    wkv = wf_ref[0] + wb_ref[0]
    mu = _mm_exact_rhs(wkv, bd, 3) * (1.0 / HEAD_DIM)
    wc = wkv - mu
    var = _mm_exact_rhs(wc * wc, bd, 2) * (1.0 / HEAD_DIM)
    o = wc * lax.rsqrt(var + GN_EPS) * gng_ref[...] + gnb_ref[...]
    ya = (o + bonus_ref[0]) * g_ref[0]
    wo = wo_ref[...]
    mix = _mm(ya, wo[:D_RWKV]) + jnp.dot(yb_ref[0], wo[D_RWKV:], preferred_element_type=F32)
    x1 = _ln(DEEPNORM_ALPHA * x0 + (1.0 + m[2:3]) * mix, l1g_ref[...], l1b_ref[...], LN_EPS)
    x1_o[0] = x1
    h2 = x1 * (1.0 + m[4:5]) + m[3:4]
    h2_o[0] = h2.astype(h2_o.dtype)

    logits = _mm3(h2, wr_ref[...]) + br_ref[...]
    lane = lax.broadcasted_iota(jnp.int32, logits.shape, 1)
    work = logits
    sel = jnp.zeros(logits.shape, F32)
    top = None
    for _ in range(TOP_K):
        mx = jnp.max(work, -1, keepdims=True)
        first = jnp.min(jnp.where(work == mx, lane, LANES), -1, keepdims=True)
        pick = lane == first
        sel = jnp.where(pick, 1.0, sel)
        work = jnp.where(pick, -jnp.inf, work)
        top = mx if top is None else top
    e = jnp.where(sel > 0.0, jnp.exp(logits - top), 0.0)
    gate_o[0] = e / jnp.sum(e, -1, keepdims=True)


def _post_call(x, mod, in_g, in_b, wkv_f, wkv_b, bonus, g, y_b, gn_g, gn_b, bd, w_out, ln1_g, ln1_b,
               w_router_p, b_router_p):
    B, T, D = x.shape
    tm = min(256, T)
    const = lambda b, t: (0, 0)
    tile = lambda w: pl.BlockSpec((1, tm, w), lambda b, t: (b, t, 0))
    vec = lambda w: pl.BlockSpec((1, w), const)
    return pl.pallas_call(
        _post_kernel,
        grid=(B, T // tm),
        in_specs=[tile(D), pl.BlockSpec((1, 6, D), lambda b, t: (b, 0, 0)), vec(D), vec(D),
                  tile(D_RWKV), tile(D_RWKV), tile(D_RWKV), tile(D_RWKV), tile(D_CONV),
                  vec(D_RWKV), vec(D_RWKV), pl.BlockSpec(bd.shape, const),
                  pl.BlockSpec(w_out.shape, const), vec(D), vec(D),
                  pl.BlockSpec(w_router_p.shape, const), vec(LANES)],
        out_specs=[tile(D), tile(D), tile(LANES)],
        out_shape=[jax.ShapeDtypeStruct((B, T, D), F32), jax.ShapeDtypeStruct((B, T, D), BF16),
                   jax.ShapeDtypeStruct((B, T, LANES), F32)],
        compiler_params=_params(("parallel", "parallel")),
        name="post",
    )(x, mod, _row(in_g), _row(in_b), wkv_f, wkv_b, bonus, g, y_b, _row(gn_g), _row(gn_b), bd, w_out,
      _row(ln1_g), _row(ln1_b), w_router_p, b_router_p)


def _moe_kernel(h_ref, gate_ref, x1_ref, mod_ref, wgu_ref, bgu_ref, wd_ref, bdn_ref, g_ref, b_ref, o_ref, acc_ref):
    e = pl.program_id(2)

    @pl.when(e == 0)
    def _():
        acc_ref[...] = jnp.zeros_like(acc_ref)

    h = jnp.dot(h_ref[0], wgu_ref[0], preferred_element_type=F32) + bgu_ref[0]
    h_glu = jnp.minimum(h[:, :D_FF], SWIGLU_LIMIT)
    h_lin = jnp.clip(h[:, D_FF:], -SWIGLU_LIMIT, SWIGLU_LIMIT)
    y = (h_lin + 1.0) * (h_glu * _sigmoid(SWIGLU_ALPHA * h_glu))
    d = _mm(y, wd_ref[0]) + bdn_ref[0]
    gates = gate_ref[0]
    lane = lax.broadcasted_iota(jnp.int32, gates.shape, 1)
    ge = jnp.sum(jnp.where(lane == e, gates, 0.0), -1, keepdims=True)
    acc_ref[...] += ge * d

    @pl.when(e == pl.num_programs(2) - 1)
    def _():
        m = mod_ref[0]
        o_ref[0] = _ln(DEEPNORM_ALPHA * x1_ref[0] + (1.0 + m[5:6]) * acc_ref[...], g_ref[...], b_ref[...], LN_EPS)


def _moe_call(h2, gates, x1, mod, w_gu, b_gu, w_down, b_down, ln2_g, ln2_b):
    B, T, D = x1.shape
    tm = min(1024, T)
    E = w_gu.shape[0]
    const = lambda b, t, e: (0, 0)
    tile = lambda w: pl.BlockSpec((1, tm, w), lambda b, t, e: (b, t, 0))
    return pl.pallas_call(
        _moe_kernel,
        grid=(B, T // tm, E),
        in_specs=[tile(D), tile(LANES), tile(D), pl.BlockSpec((1, 6, D), lambda b, t, e: (b, 0, 0)),
                  pl.BlockSpec((1, D, 2 * D_FF), lambda b, t, e: (e, 0, 0)),
                  pl.BlockSpec((1, 1, 2 * D_FF), lambda b, t, e: (e, 0, 0)),
                  pl.BlockSpec((1, D_FF, D), lambda b, t, e: (e, 0, 0)),
                  pl.BlockSpec((1, 1, D), lambda b, t, e: (e, 0, 0)),
                  pl.BlockSpec((1, D), const), pl.BlockSpec((1, D), const)],
        out_specs=tile(D),
        out_shape=jax.ShapeDtypeStruct((B, T, D), F32),
        scratch_shapes=[pltpu.VMEM((tm, D), F32)],
        compiler_params=_params(("parallel", "parallel", "arbitrary")),
        name="moe",
    )(h2, gates, x1, mod, w_gu, b_gu.reshape(E, 1, -1), w_down, b_down.reshape(E, 1, -1), _row(ln2_g), _row(ln2_b))


def _block_diag_cat(w):
    z = jnp.zeros_like(w[0])
    return jnp.concatenate([jnp.concatenate([w[0], z], 1), jnp.concatenate([z, w[1]], 1)], 0)


def _encode(x, mod, W):
    p_r, p_c = _inproj_call(x, mod, W["in_g"], W["in_b"], W["w_in_r"], W["w_in_c"])
    r, v, kk, lw, kd, bb, g, bonus = _pre_call(p_r, W["mu_shift"], W["w0c"], W["w2c"], W["a0c"], W["a2c"],
                                               W["g_up"], W["k_k"], W["k_a"], W["r_k"], W["bd"])
    wkv_f, wkv_b = _scan_call(r, v, kk, lw, kd, bb)
    y_b = _conv_call(p_c, W["dw_w"], W["dw_b"], W["cln_g"], W["cln_b"])
    x1, h2, gates = _post_call(x, mod, W["in_g"], W["in_b"], wkv_f, wkv_b, bonus, g, y_b, W["gn_g"], W["gn_b"],
                               W["bd"], W["w_out"], W["ln1_g"], W["ln1_b"], W["w_router_p"], W["b_router_p"])
    return _moe_call(h2, gates, x1, mod, W["w_gu"], W["b_gu"], W["w_down"], W["b_down"], W["ln2_g"], W["ln2_b"])


def kernel(x_prompt, x_sample, c_prompt, c_sample, in_g, in_b, w_mod, b_mod, w_in, mu_shift, w0, w2, a0, a2, g_up, k_k, k_a, r_k, gn_g, gn_b, dw_w, dw_b, cln_g, cln_b, w_out, ln1_g, ln1_b, w_router, b_router, w_gu, b_gu, w_down, b_down, ln2_g, ln2_b):
    l = 0
    head = lax.broadcasted_iota(jnp.int32, (D_RWKV, D_RWKV), 0) // HEAD_DIM
    W = dict(
        in_g=in_g, in_b=in_b,
        w_in_r=w_in[l][:, :N_SHIFT_COLS].astype(BF16), w_in_c=w_in[l][:, N_SHIFT_COLS:].astype(BF16),
        mu_shift=mu_shift[l],
        w0c=w0[l].reshape(1, -1), w2c=_block_diag_cat(w2[l]).astype(BF16),
        a0c=a0[l].reshape(1, -1), a2c=_block_diag_cat(a2[l]).astype(BF16),
        g_up=g_up[l].astype(BF16), k_k=_row(k_k[l]), k_a=_row(k_a[l]), r_k=_row(r_k[l]),
        bd=(head == head.T).astype(BF16),
        gn_g=gn_g[l], gn_b=gn_b[l], dw_w=dw_w[l], dw_b=dw_b[l], cln_g=cln_g[l], cln_b=cln_b[l],
        w_out=w_out[l].astype(BF16), ln1_g=ln1_g[l], ln1_b=ln1_b[l],
        w_router_p=jnp.pad(w_router[l], ((0, 0), (0, LANES - N_EXPERTS))),
        b_router_p=jnp.pad(b_router[l], (0, LANES - N_EXPERTS), constant_values=-1e30).reshape(1, -1),
        w_gu=w_gu[l].astype(BF16), b_gu=b_gu[l], w_down=w_down[l].astype(BF16), b_down=b_down[l],
        ln2_g=ln2_g[l], ln2_b=ln2_b[l],
    )
    nb = x_prompt.shape[0]
    mod = _mod_call(jnp.concatenate([c_prompt, c_sample], 0), w_mod[l], b_mod[l])
    return (_encode(x_prompt, mod[:nb], W), _encode(x_sample, mod[nb:], W))
```

```python
import functools

import jax
import jax.numpy as jnp
from jax import lax
from jax.experimental import pallas as pl
from jax.experimental.pallas import tpu as pltpu

F32 = jnp.float32
BF16 = jnp.bfloat16

D_MODEL = 1024
D_RWKV = 512
HEAD_DIM = 64
N_HEADS = 8
D_CONV = 512
DECAY_RANK = 64
AAA_RANK = 64
GATE_RANK = 128
CONV_WIDTH = 31
N_EXPERTS = 32
TOP_K = 4
D_FF = 1024
SWIGLU_LIMIT = 7.0
SWIGLU_ALPHA = 1.702
LN_EPS = 1e-5
GN_EPS = 64e-5
DEPTH = 1
DEEPNORM_ALPHA = (2.0 * DEPTH) ** 0.25
N_SHIFT_COLS = 3 * D_RWKV + 2 * DECAY_RANK + 2 * AAA_RANK + GATE_RANK

LANES = 128
HALO = 16
CHUNK = 64
SCAN_BLOCK = 4 * CHUNK
TOKEN_BLOCK = 512
CAP = 128
ROW_ALIGN = 16
FFN_TILE = 256
GROUP = 4
GROUP_W = GROUP * HEAD_DIM
VMEM_LIMIT = 56 * 1024 * 1024


def _params(sem, flags=None):
    return pltpu.CompilerParams(dimension_semantics=sem, vmem_limit_bytes=VMEM_LIMIT, flags=flags)


def _sigmoid(x):
    return 1.0 / (1.0 + jnp.exp(-x))


def _mm(a, b):
    return jnp.dot(a.astype(BF16), b.astype(BF16), preferred_element_type=F32)


def _mm_nt(a, b):
    return lax.dot_general(a.astype(BF16), b.astype(BF16), (((1,), (1,)), ((), ())),
                           preferred_element_type=F32)


def _mm_tn(a, b):
    return lax.dot_general(a.astype(BF16), b.astype(BF16), (((0,), (0,)), ((), ())),
                           preferred_element_type=F32)


def _split(a, n):
    parts = []
    for _ in range(n):
        h = a.astype(BF16)
        parts.append(h)
        a = a - h.astype(F32)
    return parts


def _mm_exact_rhs(a, b_bf16, n):
    acc = None
    for h in _split(a, n):
        t = jnp.dot(h, b_bf16, preferred_element_type=F32)
        acc = t if acc is None else acc + t
    return acc


def _mm_exact_lhs(a_bf16, b, n):
    acc = None
    for h in _split(b, n):
        t = jnp.dot(a_bf16, h, preferred_element_type=F32)
        acc = t if acc is None else acc + t
    return acc


def _mm3(a, b):
    ah, al = _split(a, 2)
    bh, bl = _split(b, 2)
    return (jnp.dot(ah, bh, preferred_element_type=F32) + jnp.dot(al, bh, preferred_element_type=F32)
            + jnp.dot(ah, bl, preferred_element_type=F32))


def _ln(x, g, b, eps):
    mu = jnp.mean(x, -1, keepdims=True)
    xc = x - mu
    var = jnp.mean(xc * xc, -1, keepdims=True)
    return xc * lax.rsqrt(var + eps) * g + b


def _row(a):
    return a.reshape(1, -1)


def _mod_kernel(c_ref, w_ref, b_ref, o_ref):
    c = c_ref[...]
    o_ref[...] = _mm3(c * _sigmoid(c), w_ref[...]) + b_ref[...]


def _mod_call(c, w_mod, b_mod):
    n, d = c.shape
    npad = -(-n // 8) * 8
    cp = jnp.pad(c, ((0, npad - n), (0, 0)))
    tn = 1536
    out = pl.pallas_call(
        _mod_kernel,
        grid=(6 * d // tn,),
        in_specs=[pl.BlockSpec((npad, d), lambda j: (0, 0)),
                  pl.BlockSpec((d, tn), lambda j: (0, j)),
                  pl.BlockSpec((1, tn), lambda j: (0, j))],
        out_specs=pl.BlockSpec((npad, tn), lambda j: (0, j)),
        out_shape=jax.ShapeDtypeStruct((npad, 6 * d), F32),
        compiler_params=_params(("arbitrary",)),
        name="mod",
    )(cp, w_mod, _row(b_mod))
    return out[:n].reshape(n, 6, d)


def _inproj_kernel(x_ref, mod_ref, g_ref, b_ref, wr_ref, wc_ref, pr_ref, pc_ref):
    x0 = _ln(x_ref[0], g_ref[...], b_ref[...], LN_EPS)
    m = mod_ref[0]
    h = (x0 * (1.0 + m[1:2]) + m[0:1]).astype(BF16)
    pr_ref[0] = jnp.dot(h, wr_ref[...], preferred_element_type=F32).astype(pr_ref.dtype)
    pc_ref[0] = jnp.dot(h, wc_ref[...], preferred_element_type=F32).astype(pc_ref.dtype)


def _inproj_call(x, mod, in_g, in_b, w_r, w_c):
    B, T, D = x.shape
    tm = min(512, T)
    const = lambda b, t: (0, 0)
    return pl.pallas_call(
        _inproj_kernel,
        grid=(B, T // tm),
        in_specs=[pl.BlockSpec((1, tm, D), lambda b, t: (b, t, 0)),
                  pl.BlockSpec((1, 6, D), lambda b, t: (b, 0, 0)),
                  pl.BlockSpec((1, D), const), pl.BlockSpec((1, D), const),
                  pl.BlockSpec(w_r.shape, const), pl.BlockSpec(w_c.shape, const)],
        out_specs=[pl.BlockSpec((1, tm, N_SHIFT_COLS), lambda b, t: (b, t, 0)),
                   pl.BlockSpec((1, tm, 2 * D_CONV), lambda b, t: (b, t, 0))],
        out_shape=[jax.ShapeDtypeStruct((B, T, N_SHIFT_COLS), BF16),
                   jax.ShapeDtypeStruct((B, T, 2 * D_CONV), BF16)],
        compiler_params=_params(("parallel", "parallel")),
        name="inproj",
    )(x, mod, _row(in_g), _row(in_b), w_r, w_c)


def _pre_kernel(p_ref, pp_ref, pn_ref, mu_ref, w0_ref, w2_ref, a0_ref, a2_ref, gup_ref, kkw_ref, ka_ref,
                rk_ref, bd_ref, r_o, v_o, kk_o, lw_o, kd_o, bb_o, g_o, bonus_o):
    t = pl.program_id(1)
    nt = pl.num_programs(1)
    p = p_ref[0].astype(F32)
    tm = p.shape[0]
    prev_row = jnp.where(t > 0, pp_ref[0].astype(F32)[HALO - 1:HALO], 0.0)
    next_row = jnp.where(t < nt - 1, pn_ref[0].astype(F32)[0:1], 0.0)
    row = lax.broadcasted_iota(jnp.int32, p.shape, 0)
    prev = jnp.where(row == 0, prev_row, pltpu.roll(p, 1, 0))
    nxt = jnp.where(row == tm - 1, next_row, pltpu.roll(p, tm - 1, 0))
    mu = mu_ref[...]
    ps = p + mu[0:1] * (prev - p) + mu[1:2] * (nxt - p)

    r = ps[:, 0:D_RWKV]
    k = ps[:, D_RWKV:2 * D_RWKV]
    v = ps[:, 2 * D_RWKV:3 * D_RWKV]
    o = 3 * D_RWKV
    wl = ps[:, o:o + 2 * DECAY_RANK]
    al = ps[:, o + 2 * DECAY_RANK:o + 2 * DECAY_RANK + 2 * AAA_RANK]
    gl = ps[:, o + 2 * DECAY_RANK + 2 * AAA_RANK:]

    y = -(w0_ref[...] + _mm(jnp.tanh(wl), w2_ref[...]))
    softplus = jnp.maximum(y, 0.0) + jnp.log(1.0 + jnp.exp(-jnp.abs(y)))
    lw = -jnp.exp(-softplus - 0.5)
    a = _sigmoid(a0_ref[...] + _mm(al, a2_ref[...]))
    g = _mm(_sigmoid(gl), gup_ref[...])

    bd = bd_ref[...]
    kk = k * kkw_ref[...]
    ss = _mm_exact_rhs(kk * kk, bd, 2)
    kk = kk * lax.rsqrt(jnp.maximum(ss, 1e-24))
    ka = ka_ref[...]
    a0d, a1d = a[:, :D_RWKV], a[:, D_RWKV:]
    k0 = k * (1.0 + (a0d - 1.0) * ka)
    k1 = k * (1.0 + (a1d - 1.0) * ka)
    bonus = _mm_exact_rhs(r * (k0 + k1) * rk_ref[...], bd, 2) * v

    r_o[0] = r
    v_o[0] = v
    kk_o[0] = kk
    lw_o[0] = lw
    kd_o[0, :, :D_RWKV] = k0
    kd_o[0, :, D_RWKV:] = k1
    bb_o[0, :, :D_RWKV] = kk * a0d
    bb_o[0, :, D_RWKV:] = kk * a1d
    g_o[0] = g
    bonus_o[0] = bonus


def _pre_call(p_r, mu_shift, w0c, w2c, a0c, a2c, g_up, k_k, k_a, r_k, bd):
    B, T, C = p_r.shape
    tm = min(256, T)
    nh = tm // HALO
    const = lambda b, t: (0, 0)
    tile = lambda w: pl.BlockSpec((1, tm, w), lambda b, t: (b, t, 0))
    outs = [D_RWKV, D_RWKV, D_RWKV, 2 * D_RWKV, 2 * D_RWKV, 2 * D_RWKV, D_RWKV, D_RWKV]
    return pl.pallas_call(
        _pre_kernel,
        grid=(B, T // tm),
        in_specs=[tile(C),
                  pl.BlockSpec((1, HALO, C), lambda b, t: (b, jnp.maximum(t * nh - 1, 0), 0)),
                  pl.BlockSpec((1, HALO, C), lambda b, t: (b, jnp.minimum((t + 1) * nh, T // HALO - 1), 0)),
                  pl.BlockSpec(mu_shift.shape, const), pl.BlockSpec(w0c.shape, const),
                  pl.BlockSpec(w2c.shape, const), pl.BlockSpec(a0c.shape, const),
                  pl.BlockSpec(a2c.shape, const), pl.BlockSpec(g_up.shape, const),
                  pl.BlockSpec(k_k.shape, const), pl.BlockSpec(k_a.shape, const),
                  pl.BlockSpec(r_k.shape, const), pl.BlockSpec(bd.shape, const)],
        out_specs=[tile(w) for w in outs],
        out_shape=[jax.ShapeDtypeStruct((B, T, w), F32) for w in outs],
        compiler_params=_params(("parallel", "parallel")),
        name="pre",
    )(p_r, p_r, p_r, mu_shift, w0c, w2c, a0c, a2c, g_up, k_k, k_a, r_k, bd)


def _chains_chunk(chains):
    L = chains[0]["rt"].shape[0]
    R = GROUP * L
    rowh = lax.broadcasted_iota(jnp.int32, (R, GROUP_W), 0) // L
    colh = lax.broadcasted_iota(jnp.int32, (R, GROUP_W), 1) // HEAD_DIM
    own = rowh == colh

    def stack(x):
        return jnp.concatenate([x] * GROUP, axis=0)

    def stack_own(x):
        return jnp.where(own, stack(x), 0.0).astype(BF16)

    rr = lax.broadcasted_iota(jnp.int32, (R, R), 0)
    cc = lax.broadcasted_iota(jnp.int32, (R, R), 1)
    tl = rr % L
    eye = jnp.where(rr == cc, 1.0, 0.0)
    masks = {}
    for reverse in sorted({c["reverse"] for c in chains}):
        delta, span = (cc - rr, (L - 1) - tl) if reverse else (rr - cc, tl)
        masks[reverse] = ((delta - 1).astype(jnp.uint32) < span.astype(jnp.uint32),
                          delta.astype(jnp.uint32) <= span.astype(jnp.uint32))

    zr = [jnp.concatenate([stack_own(c["zt"]), stack_own(c["rt"])], axis=0) for c in chains]
    bk = [jnp.concatenate([stack(c["bt"]), stack(c["kt"])], axis=0).astype(BF16) for c in chains]
    a_all = [_mm_nt(x, y) for x, y in zip(zr, bk)]
    n_zb = [jnp.where(masks[c["reverse"]][0], a[:R, :R], 0.0) for c, a in zip(chains, a_all)]
    a_zr = [jnp.concatenate([jnp.where(masks[c["reverse"]][0], a[:R, R:], 0.0),
                             jnp.where(masks[c["reverse"]][1], a[R:, R:], 0.0)], axis=0).astype(BF16)
            for c, a in zip(chains, a_all)]
    a_rb = [jnp.where(masks[c["reverse"]][1], a[R:, :R], 0.0).astype(BF16) for c, a in zip(chains, a_all)]

    tinv = [eye + n for n in n_zb]
    pw = [_mm(n, n).astype(BF16) for n in n_zb]
    levels = L.bit_length() - 2
    for j in range(1, levels + 1):
        if j < levels:
            both = [jnp.dot(jnp.concatenate([p, t.astype(BF16)], axis=0), p, preferred_element_type=F32)
                    for p, t in zip(pw, tinv)]
            pw = [b[:R].astype(BF16) for b in both]
            tinv = [t + b[R:] for t, b in zip(tinv, both)]
        else:
            tinv = [t + jnp.dot(t.astype(BF16), p, preferred_element_type=F32) for p, t in zip(pw, tinv)]

    v4 = [jnp.concatenate([c["v"][:, h * HEAD_DIM:(h + 1) * HEAD_DIM] for h in range(GROUP)], axis=0)
          for c in chains]
    zs = [_mm_nt(x, c["S"]) for x, c in zip(zr, chains)]
    av = [_mm(a, x) for a, x in zip(a_zr, v4)]
    u = [_mm(t, z[:R] + a[:R]) for t, z, a in zip(tinv, zs, av)]
    o4 = [z[R:] + a[R:] + _mm(ar, x) for z, a, ar, x in zip(zs, av, a_rb, u)]
    outs = [jnp.concatenate([o[h * L:(h + 1) * L] for h in range(GROUP)], axis=1) for o in o4]
    states = [c["S"] * c["wtot"] + _mm_tn(x, stack_own(c["bhat"])) + _mm_tn(y, stack_own(c["khat"]))
              for c, x, y in zip(chains, u, v4)]
    return outs, states


def _prep_chains(r, v, kk, lw, kd, bb, S, reverse):
    L = r.shape[0]
    ti = lax.broadcasted_iota(jnp.int32, (L, L), 0)
    ii = lax.broadcasted_iota(jnp.int32, (L, L), 1)
    tri = jnp.where((ii >= ti) if reverse else (ii <= ti), 1.0, 0.0).astype(BF16)
    cum = _mm_exact_lhs(tri, lw, 3)
    tot = cum[0:1] if reverse else cum[L - 1:L]
    ec, eci, ecx, eto = jnp.exp(cum), jnp.exp(-cum), jnp.exp(cum - lw), jnp.exp(tot - cum)
    full = dict(rt=r * ec, zt=-(kk * ecx), kt=kd * eci, bt=bb * eci, khat=kd * eto, bhat=bb * eto, v=v, S=S,
                wtot=jnp.exp(tot))
    chains = []
    for g in range(D_RWKV // GROUP_W):
        c = {k: x[:, g * GROUP_W:(g + 1) * GROUP_W] for k, x in full.items()}
        c["reverse"] = reverse
        chains.append(c)
    return chains


def _scan_kernel(rf, vf, kkf, lwf, kdf, bbf, rb, vb, kkb, lwb, kdb, bbb, of, ob, s_ref):
    @pl.when(pl.program_id(1) == 0)
    def _():
        s_ref[...] = jnp.zeros_like(s_ref)

    n_sub = rf.shape[1] // CHUNK
    n_grp = D_RWKV // GROUP_W
    sf, sb = s_ref[0], s_ref[1]
    for i in range(n_sub):
        lf = pl.ds(i * CHUNK, CHUNK)
        lb = pl.ds((n_sub - 1 - i) * CHUNK, CHUNK)
        chains = (_prep_chains(rf[0, lf], vf[0, lf], kkf[0, lf], lwf[0, lf], kdf[0, lf], bbf[0, lf], sf, False)
                  + _prep_chains(rb[0, lb], vb[0, lb], kkb[0, lb], lwb[0, lb], kdb[0, lb], bbb[0, lb], sb, True))
        outs, states = _chains_chunk(chains)
        of[0, lf] = jnp.concatenate(outs[:n_grp], axis=1)
        ob[0, lb] = jnp.concatenate(outs[n_grp:], axis=1)
        sf = jnp.concatenate(states[:n_grp], axis=1)
        sb = jnp.concatenate(states[n_grp:], axis=1)
    s_ref[0] = sf
    s_ref[1] = sb


def _scan_call(r, v, kk, lw, kd, bb):
    B, T, C = r.shape
    L = min(SCAN_BLOCK, T)
    nc = T // L
    fwd = lambda j: pl.BlockSpec((1, L, C), lambda b, c: (b, c, j))
    bwd = lambda j: pl.BlockSpec((1, L, C), lambda b, c: (b, nc - 1 - c, j))
    return pl.pallas_call(
        _scan_kernel,
        grid=(B, nc),
        in_specs=[fwd(0)] * 6 + [bwd(0)] * 3 + [bwd(1)] * 3,
        out_specs=[fwd(0), bwd(0)],
        out_shape=[jax.ShapeDtypeStruct((B, T, C), F32)] * 2,
        scratch_shapes=[pltpu.VMEM((2, HEAD_DIM, C), F32)],
        compiler_params=_params(("parallel", "arbitrary")),
        name="scan",
    )(r, v, kk, lw, kd, bb, r, v, kk, lw, kd, bb)


def _conv_kernel(p_ref, pp_ref, pn_ref, w_ref, b_ref, g_ref, be_ref, o_ref, h_ref):
    t = pl.program_id(1)
    nt = pl.num_programs(1)
    tm = p_ref.shape[1]

    def glu(x):
        x = x.astype(F32)
        return x[:, :D_CONV] * _sigmoid(x[:, D_CONV:])

    h_ref[0:HALO] = jnp.where(t > 0, glu(pp_ref[0]), 0.0)
    h_ref[HALO:HALO + tm] = glu(p_ref[0])
    h_ref[HALO + tm:] = jnp.where(t < nt - 1, glu(pn_ref[0]), 0.0)
    w = w_ref[...]
    acc = jnp.zeros((tm, D_CONV), F32) + b_ref[...]
    base = HALO - CONV_WIDTH // 2
    for j in range(CONV_WIDTH):
        acc = acc + w[j:j + 1] * h_ref[base + j:base + j + tm]
    y = _ln(acc, g_ref[...], be_ref[...], LN_EPS)
    o_ref[0] = (y * _sigmoid(y)).astype(o_ref.dtype)


def _conv_call(p_c, dw_w, dw_b, cln_g, cln_b):
    B, T, C = p_c.shape
    tm = min(256, T)
    nh = tm // HALO
    const = lambda b, t: (0, 0)
    return pl.pallas_call(
        _conv_kernel,
        grid=(B, T // tm),
        in_specs=[pl.BlockSpec((1, tm, C), lambda b, t: (b, t, 0)),
                  pl.BlockSpec((1, HALO, C), lambda b, t: (b, jnp.maximum(t * nh - 1, 0), 0)),
                  pl.BlockSpec((1, HALO, C), lambda b, t: (b, jnp.minimum((t + 1) * nh, T // HALO - 1), 0)),
                  pl.BlockSpec(dw_w.shape, const), pl.BlockSpec((1, D_CONV), const),
                  pl.BlockSpec((1, D_CONV), const), pl.BlockSpec((1, D_CONV), const)],
        out_specs=pl.BlockSpec((1, tm, D_CONV), lambda b, t: (b, t, 0)),
        out_shape=jax.ShapeDtypeStruct((B, T, D_CONV), BF16),
        scratch_shapes=[pltpu.VMEM((tm + 2 * HALO, D_CONV), F32)],
        compiler_params=_params(("parallel", "parallel")),
        name="conv",
    )(p_c, p_c, p_c, dw_w, _row(dw_b), _row(cln_g), _row(cln_b))


def _post_kernel(x_ref, mod_ref, ing_ref, inb_ref, wf_ref, wb_ref, bonus_ref, g_ref, yb_ref, gng_ref, gnb_ref,
                 bd_ref, wo_ref, l1g_ref, l1b_ref, wr_ref, br_ref, x1_o, h2_o, gate_o):
    m = mod_ref[0]
    x0 = _ln(x_ref[0], ing_ref[...], inb_ref[...], LN_EPS)
    bd = bd_ref[...]
    wkv = wf_ref[0] + wb_ref[0]
    mu = _mm_exact_rhs(wkv, bd, 3) * (1.0 / HEAD_DIM)
    wc = wkv - mu
    var = _mm_exact_rhs(wc * wc, bd, 2) * (1.0 / HEAD_DIM)
    o = wc * lax.rsqrt(var + GN_EPS) * gng_ref[...] + gnb_ref[...]
    ya = (o + bonus_ref[0]) * g_ref[0]
    wo = wo_ref[...]
    mix = _mm(ya, wo[:D_RWKV]) + jnp.dot(yb_ref[0], wo[D_RWKV:], preferred_element_type=F32)
    x1 = _ln(DEEPNORM_ALPHA * x0 + (1.0 + m[2:3]) * mix, l1g_ref[...], l1b_ref[...], LN_EPS)
    x1_o[0] = x1
    h2 = x1 * (1.0 + m[4:5]) + m[3:4]
    h2_o[0] = h2.astype(h2_o.dtype)

    logits = _mm3(h2, wr_ref[...]) + br_ref[...]
    lane = lax.broadcasted_iota(jnp.int32, logits.shape, 1)
    work = logits
    sel = jnp.zeros(logits.shape, F32)
    top = None
    for _ in range(TOP_K):
        mx = jnp.max(work, -1, keepdims=True)
        first = jnp.min(jnp.where(work == mx, lane, LANES), -1, keepdims=True)
        pick = lane == first
        sel = jnp.where(pick, 1.0, sel)
        work = jnp.where(pick, -jnp.inf, work)
        top = mx if top is None else top
    e = jnp.where(sel > 0.0, jnp.exp(logits - top), 0.0)
    gates = e / jnp.sum(e, -1, keepdims=True)
    gate_o[0] = gates.T[:N_EXPERTS]


def _post_call(x, mod, in_g, in_b, wkv_f, wkv_b, bonus, g, y_b, gn_g, gn_b, bd, w_out, ln1_g, ln1_b,
               w_router_p, b_router_p):
    B, T, D = x.shape
    tm = min(256, T)
    const = lambda b, t: (0, 0)
    tile = lambda w: pl.BlockSpec((1, tm, w), lambda b, t: (b, t, 0))
    vec = lambda w: pl.BlockSpec((1, w), const)
    return pl.pallas_call(
        _post_kernel,
        grid=(B, T // tm),
        in_specs=[tile(D), pl.BlockSpec((1, 6, D), lambda b, t: (b, 0, 0)), vec(D), vec(D),
                  tile(D_RWKV), tile(D_RWKV), tile(D_RWKV), tile(D_RWKV), tile(D_CONV),
                  vec(D_RWKV), vec(D_RWKV), pl.BlockSpec(bd.shape, const),
                  pl.BlockSpec(w_out.shape, const), vec(D), vec(D),
                  pl.BlockSpec(w_router_p.shape, const), vec(LANES)],
        out_specs=[tile(D), tile(D), pl.BlockSpec((1, N_EXPERTS, tm), lambda b, t: (b, 0, t))],
        out_shape=[jax.ShapeDtypeStruct((B, T, D), F32), jax.ShapeDtypeStruct((B, T, D), BF16),
                   jax.ShapeDtypeStruct((B, N_EXPERTS, T), F32)],
        compiler_params=_params(("parallel", "parallel")),
        name="post",
    )(x, mod, _row(in_g), _row(in_b), wkv_f, wkv_b, bonus, g, y_b, _row(gn_g), _row(gn_b), bd, w_out,
      _row(ln1_g), _row(ln1_b), w_router_p, b_router_p)


def _routing(gates_t, tb):
    B, E, T = gates_t.shape
    nb = T // tb
    n_blocks = B * nb
    cnt = jnp.sum((gates_t > 0.0).reshape(B, E, nb, tb), -1, dtype=jnp.int32)
    cnt = cnt.transpose(1, 0, 2).reshape(E, n_blocks)
    cnt_al = (cnt + ROW_ALIGN - 1) // ROW_ALIGN * ROW_ALIGN
    used = jnp.sum(cnt_al, 1) + CAP
    seg = (used + FFN_TILE - 1) // FFN_TILE * FFN_TILE
    seg_end = jnp.cumsum(seg)
    seg_start = seg_end - seg
    start = seg_start[:, None] + jnp.cumsum(cnt_al, 1) - cnt_al
    rows_max = TOP_K * B * T + (ROW_ALIGN - 1) * n_blocks * E + E * (CAP + FFN_TILE - 1)
    n_tiles = -(-rows_max // FFN_TILE)
    idx = jnp.arange(n_tiles, dtype=jnp.int32)
    tile_e = jnp.minimum(jnp.searchsorted(seg_end, idx * FFN_TILE, side="right"), E - 1).astype(jnp.int32)
    valid = idx * FFN_TILE < (seg_start + used)[tile_e]
    src = lax.cummax(jnp.where(valid, idx, 0))
    over = jnp.max(cnt, 0) > CAP
    return dict(start=start.reshape(-1), cnt=cnt.reshape(-1), over=over.astype(jnp.int32),
                tile_e=tile_e[src], src=src, valid=valid.astype(jnp.int32), n_tiles=n_tiles)


def _block_ranks(g, tri):
    routed = g > 0.0
    rank = jnp.dot(jnp.where(routed, 1.0, 0.0).astype(BF16), tri, preferred_element_type=F32)
    return jnp.where(routed, rank, -1.0)


def _group_copy(buf, hbm, sem, slot, row):
    return pltpu.make_async_copy(buf.at[pl.ds(slot * CAP, CAP)],
                                 hbm.at[pl.ds(pl.multiple_of(row, ROW_ALIGN), CAP)], sem.at[0])


def _dispatch_kernel(start_ref, cnt_ref, over_ref, h_ref, g_ref, tri_ref, xs_in, xs_out, stage, rank_s, sem):
    del xs_in
    E, tb = g_ref.shape[1], g_ref.shape[2]
    n_blocks = pl.num_programs(0) * pl.num_programs(1)
    blk = pl.program_id(0) * pl.num_programs(1) + pl.program_id(1)
    rank = _block_ranks(g_ref[0], tri_ref[...])
    rank_s[...] = rank
    slot = lax.broadcasted_iota(jnp.int32, (CAP, tb), 0).astype(F32)
    h = h_ref[0]
    n_part = 4
    per = E // n_part
    for q in range(n_part):
        sel = jnp.concatenate([jnp.where(rank[e:e + 1] == slot, 1.0, 0.0).astype(BF16)
                               for e in range(q * per, (q + 1) * per)], axis=0)
        stage[q * per * CAP:(q + 1) * per * CAP] = jnp.dot(sel, h, preferred_element_type=F32).astype(BF16)
        for e in range(q * per, (q + 1) * per):
            _group_copy(stage, xs_out, sem, e, start_ref[e * n_blocks + blk]).start()
    for e in range(E):
        _group_copy(stage, xs_out, sem, e, start_ref[e * n_blocks + blk]).wait()

    @pl.when(over_ref[blk] > 0)
    def _():
        def per_expert(e, carry):
            def per_chunk(c, carry):
                row = rank_s[pl.ds(e, 1), :] - (c * CAP).astype(F32)
                sel = jnp.where(row == slot, 1.0, 0.0).astype(BF16)
                stage[0:CAP] = jnp.dot(sel, h, preferred_element_type=F32).astype(BF16)
                cp = _group_copy(stage, xs_out, sem, 0, start_ref[e * n_blocks + blk] + c * CAP)
                cp.start()
                cp.wait()
                return carry
            n_chunks = (cnt_ref[e * n_blocks + blk] + CAP - 1) // CAP
            return lax.fori_loop(1, n_chunks, per_chunk, carry)
        lax.fori_loop(0, E, per_expert, 0)


def _dispatch_call(h2, gates_t, tri, rt):
    B, T, D = h2.shape
    E = gates_t.shape[1]
    tb = tri.shape[0]
    rows = rt["n_tiles"] * FFN_TILE
    xs0 = jnp.zeros((rows, D), BF16)
    grid_spec = pltpu.PrefetchScalarGridSpec(
        num_scalar_prefetch=3,
        grid=(B, T // tb),
        in_specs=[pl.BlockSpec((1, tb, D), lambda b, t, *_: (b, t, 0)),
                  pl.BlockSpec((1, E, tb), lambda b, t, *_: (b, 0, t)),
                  pl.BlockSpec((tb, tb), lambda b, t, *_: (0, 0)),
                  pl.BlockSpec(memory_space=pl.ANY)],
        out_specs=pl.BlockSpec(memory_space=pl.ANY),
        scratch_shapes=[pltpu.VMEM((E * CAP, D), BF16), pltpu.VMEM((E, tb), F32), pltpu.SemaphoreType.DMA((1,))],
    )
    return pl.pallas_call(
        _dispatch_kernel, grid_spec=grid_spec,
        out_shape=jax.ShapeDtypeStruct((rows, D), BF16),
        input_output_aliases={6: 0},
        compiler_params=_params(("arbitrary", "arbitrary")),
        name="dispatch",
    )(rt["start"], rt["cnt"], rt["over"], h2, gates_t, tri, xs0)


def _ffn_kernel(te_ref, src_ref, valid_ref, x_ref, wgu_ref, bgu_ref, wd_ref, bdn_ref, o_ref):
    i = pl.program_id(0)

    @pl.when(valid_ref[i] > 0)
    def _():
        h = jnp.dot(x_ref[...], wgu_ref[0], preferred_element_type=F32) + bgu_ref[0]
        h_glu = jnp.minimum(h[:, :D_FF], SWIGLU_LIMIT)
        h_lin = jnp.clip(h[:, D_FF:], -SWIGLU_LIMIT, SWIGLU_LIMIT)
        y = (h_lin + 1.0) * (h_glu * _sigmoid(SWIGLU_ALPHA * h_glu))
        o_ref[...] = (_mm(y, wd_ref[0]) + bdn_ref[0]).astype(o_ref.dtype)

    @pl.when(valid_ref[i] == 0)
    def _():
        o_ref[...] = jnp.zeros_like(o_ref)


def _ffn_call(xs, rt, w_gu, b_gu, w_down, b_down):
    rows, D = xs.shape
    E = w_gu.shape[0]
    grid_spec = pltpu.PrefetchScalarGridSpec(
        num_scalar_prefetch=3,
        grid=(rt["n_tiles"],),
        in_specs=[pl.BlockSpec((FFN_TILE, D), lambda i, te, src, valid: (src[i], 0)),
                  pl.BlockSpec((1, D, 2 * D_FF), lambda i, te, src, valid: (te[i], 0, 0)),
                  pl.BlockSpec((1, 1, 2 * D_FF), lambda i, te, src, valid: (te[i], 0, 0)),
                  pl.BlockSpec((1, D_FF, D), lambda i, te, src, valid: (te[i], 0, 0)),
                  pl.BlockSpec((1, 1, D), lambda i, te, src, valid: (te[i], 0, 0))],
        out_specs=pl.BlockSpec((FFN_TILE, D), lambda i, te, src, valid: (i, 0)),
    )
    return pl.pallas_call(
        _ffn_kernel, grid_spec=grid_spec,
        out_shape=jax.ShapeDtypeStruct((rows, D), BF16),
        compiler_params=_params(("arbitrary",)),
        name="ffn",
    )(rt["tile_e"], rt["src"], rt["valid"], xs, w_gu, b_gu.reshape(E, 1, -1), w_down,
      b_down.reshape(E, 1, -1))


def _combine_kernel(start_ref, cnt_ref, over_ref, g_ref, x1_ref, mod_ref, tri_ref, lg_ref, lb_ref, ys_ref, o_ref,
                    ybuf, rank_s, acc_s, sem):
    E, tb = g_ref.shape[1], g_ref.shape[2]
    n_blocks = pl.num_programs(0) * pl.num_programs(1)
    blk = pl.program_id(0) * pl.num_programs(1) + pl.program_id(1)

    def fetch(slot, row):
        return pltpu.make_async_copy(ys_ref.at[pl.ds(pl.multiple_of(row, ROW_ALIGN), CAP)],
                                     ybuf.at[pl.ds(slot * CAP, CAP)], sem.at[0])

    for e in range(E):
        fetch(e, start_ref[e * n_blocks + blk]).start()
    g = g_ref[0]
    rank = _block_ranks(g, tri_ref[...])
    slot = lax.broadcasted_iota(jnp.int32, (CAP, tb), 0).astype(F32)
    for e in range(E):
        fetch(e, start_ref[e * n_blocks + blk]).wait()
    n_part = 4
    per = E // n_part
    acc = None
    for q in range(n_part):
        w = jnp.concatenate([jnp.where(rank[e:e + 1] == slot, g[e:e + 1], 0.0).astype(BF16)
                             for e in range(q * per, (q + 1) * per)], axis=0)
        part = _mm_tn(w, ybuf[q * per * CAP:(q + 1) * per * CAP])
        acc = part if acc is None else acc + part
    acc_s[...] = acc

    @pl.when(over_ref[blk] > 0)
    def _():
        rank_s[...] = rank

        def per_expert(e, carry):
            def per_chunk(c, carry):
                cp = fetch(0, start_ref[e * n_blocks + blk] + c * CAP)
                cp.start()
                row = rank_s[pl.ds(e, 1), :] - (c * CAP).astype(F32)
                w = jnp.where(row == slot, g_ref[0, pl.ds(e, 1), :], 0.0).astype(BF16)
                cp.wait()
                acc_s[...] += _mm_tn(w, ybuf[0:CAP])
                return carry
            n_chunks = (cnt_ref[e * n_blocks + blk] + CAP - 1) // CAP
            return lax.fori_loop(1, n_chunks, per_chunk, carry)
        lax.fori_loop(0, E, per_expert, 0)

    m = mod_ref[0]
    o_ref[0] = _ln(DEEPNORM_ALPHA * x1_ref[0] + (1.0 + m[5:6]) * acc_s[...], lg_ref[...], lb_ref[...], LN_EPS)


def _combine_call(ys, gates_t, x1, mod, tri, rt, ln2_g, ln2_b):
    B, T, D = x1.shape
    E = gates_t.shape[1]
    tb = tri.shape[0]
    const = lambda b, t, *_: (0, 0)
    grid_spec = pltpu.PrefetchScalarGridSpec(
        num_scalar_prefetch=3,
        grid=(B, T // tb),
        in_specs=[pl.BlockSpec((1, E, tb), lambda b, t, *_: (b, 0, t)),
                  pl.BlockSpec((1, tb, D), lambda b, t, *_: (b, t, 0)),
                  pl.BlockSpec((1, 6, D), lambda b, t, *_: (b, 0, 0)),
                  pl.BlockSpec((tb, tb), const),
                  pl.BlockSpec((1, D), const), pl.BlockSpec((1, D), const),
                  pl.BlockSpec(memory_space=pl.ANY)],
        out_specs=pl.BlockSpec((1, tb, D), lambda b, t, *_: (b, t, 0)),
        scratch_shapes=[pltpu.VMEM((E * CAP, D), BF16), pltpu.VMEM((E, tb), F32), pltpu.VMEM((tb, D), F32),
                        pltpu.SemaphoreType.DMA((1,))],
    )
    return pl.pallas_call(
        _combine_kernel, grid_spec=grid_spec,
        out_shape=jax.ShapeDtypeStruct((B, T, D), F32),
        compiler_params=_params(("arbitrary", "arbitrary")),
        name="combine",
    )(rt["start"], rt["cnt"], rt["over"], gates_t, x1, mod, tri, _row(ln2_g), _row(ln2_b), ys)


def _moe_call(h2, gates_t, x1, mod, w_gu, b_gu, w_down, b_down, ln2_g, ln2_b):
    tb = min(TOKEN_BLOCK, h2.shape[1])
    pos = lax.broadcasted_iota(jnp.int32, (tb, tb), 0)
    tri = (pos < pos.T).astype(BF16)
    rt = _routing(gates_t, tb)
    xs = _dispatch_call(h2, gates_t, tri, rt)
    ys = _ffn_call(xs, rt, w_gu, b_gu, w_down, b_down)
    return _combine_call(ys, gates_t, x1, mod, tri, rt, ln2_g, ln2_b)


def _block_diag_cat(w):
    z = jnp.zeros_like(w[0])
    return jnp.concatenate([jnp.concatenate([w[0], z], 1), jnp.concatenate([z, w[1]], 1)], 0)


def _encode(x, mod, W):
    p_r, p_c = _inproj_call(x, mod, W["in_g"], W["in_b"], W["w_in_r"], W["w_in_c"])
    r, v, kk, lw, kd, bb, g, bonus = _pre_call(p_r, W["mu_shift"], W["w0c"], W["w2c"], W["a0c"], W["a2c"],
                                               W["g_up"], W["k_k"], W["k_a"], W["r_k"], W["bd"])
    wkv_f, wkv_b = _scan_call(r, v, kk, lw, kd, bb)
    y_b = _conv_call(p_c, W["dw_w"], W["dw_b"], W["cln_g"], W["cln_b"])
    x1, h2, gates = _post_call(x, mod, W["in_g"], W["in_b"], wkv_f, wkv_b, bonus, g, y_b, W["gn_g"], W["gn_b"],
                               W["bd"], W["w_out"], W["ln1_g"], W["ln1_b"], W["w_router_p"], W["b_router_p"])
    return _moe_call(h2, gates, x1, mod, W["w_gu"], W["b_gu"], W["w_down"], W["b_down"], W["ln2_g"], W["ln2_b"])


def kernel(x_prompt, x_sample, c_prompt, c_sample, in_g, in_b, w_mod, b_mod, w_in, mu_shift, w0, w2, a0, a2, g_up, k_k, k_a, r_k, gn_g, gn_b, dw_w, dw_b, cln_g, cln_b, w_out, ln1_g, ln1_b, w_router, b_router, w_gu, b_gu, w_down, b_down, ln2_g, ln2_b):
    l = 0
    head = lax.broadcasted_iota(jnp.int32, (D_RWKV, D_RWKV), 0) // HEAD_DIM
    W = dict(
        in_g=in_g, in_b=in_b,
        w_in_r=w_in[l][:, :N_SHIFT_COLS].astype(BF16), w_in_c=w_in[l][:, N_SHIFT_COLS:].astype(BF16),
        mu_shift=mu_shift[l],
        w0c=w0[l].reshape(1, -1), w2c=_block_diag_cat(w2[l]).astype(BF16),
        a0c=a0[l].reshape(1, -1), a2c=_block_diag_cat(a2[l]).astype(BF16),
        g_up=g_up[l].astype(BF16), k_k=_row(k_k[l]), k_a=_row(k_a[l]), r_k=_row(r_k[l]),
        bd=(head == head.T).astype(BF16),
        gn_g=gn_g[l], gn_b=gn_b[l], dw_w=dw_w[l], dw_b=dw_b[l], cln_g=cln_g[l], cln_b=cln_b[l],
        w_out=w_out[l].astype(BF16), ln1_g=ln1_g[l], ln1_b=ln1_b[l],
        w_router_p=jnp.pad(w_router[l], ((0, 0), (0, LANES - N_EXPERTS))),
        b_router_p=jnp.pad(b_router[l], (0, LANES - N_EXPERTS), constant_values=-1e30).reshape(1, -1),
        w_gu=w_gu[l].astype(BF16), b_gu=b_gu[l], w_down=w_down[l].astype(BF16), b_down=b_down[l],
        ln2_g=ln2_g[l], ln2_b=ln2_b[l],
    )
    nb = x_prompt.shape[0]
    mod = _mod_call(jnp.concatenate([c_prompt, c_sample], 0), w_mod[l], b_mod[l])
    return (_encode(x_prompt, mod[:nb], W), _encode(x_sample, mod[nb:], W))
```

```python
import functools

import jax
import jax.numpy as jnp
from jax import lax
from jax.experimental import pallas as pl
from jax.experimental.pallas import tpu as pltpu

F32 = jnp.float32
BF16 = jnp.bfloat16

D_MODEL = 1024
D_RWKV = 512
HEAD_DIM = 64
N_HEADS = 8
D_CONV = 512
DECAY_RANK = 64
AAA_RANK = 64
GATE_RANK = 128
CONV_WIDTH = 31
N_EXPERTS = 32
TOP_K = 4
D_FF = 1024
SWIGLU_LIMIT = 7.0
SWIGLU_ALPHA = 1.702
LN_EPS = 1e-5
GN_EPS = 64e-5
DEPTH = 1
DEEPNORM_ALPHA = (2.0 * DEPTH) ** 0.25
N_SHIFT_COLS = 3 * D_RWKV + 2 * DECAY_RANK + 2 * AAA_RANK + GATE_RANK

LANES = 128
HALO = 16
CHUNK = 64
SCAN_BLOCK = 4 * CHUNK
TOKEN_BLOCK = 512
CAP = 128
ROW_ALIGN = 16
FFN_TILE = 512
GROUP = 4
GROUP_W = GROUP * HEAD_DIM
VMEM_LIMIT = 56 * 1024 * 1024


def _params(sem, flags=None):
    return pltpu.CompilerParams(dimension_semantics=sem, vmem_limit_bytes=VMEM_LIMIT, flags=flags)


def _sigmoid(x):
    return 1.0 / (1.0 + jnp.exp(-x))


def _mm(a, b):
    return jnp.dot(a.astype(BF16), b.astype(BF16), preferred_element_type=F32)


def _mm_nt(a, b):
    return lax.dot_general(a.astype(BF16), b.astype(BF16), (((1,), (1,)), ((), ())),
                           preferred_element_type=F32)


def _mm_tn(a, b):
    return lax.dot_general(a.astype(BF16), b.astype(BF16), (((0,), (0,)), ((), ())),
                           preferred_element_type=F32)


def _split(a, n):
    parts = []
    for _ in range(n):
        h = a.astype(BF16)
        parts.append(h)
        a = a - h.astype(F32)
    return parts


def _mm_exact_rhs(a, b_bf16, n):
    acc = None
    for h in _split(a, n):
        t = jnp.dot(h, b_bf16, preferred_element_type=F32)
        acc = t if acc is None else acc + t
    return acc


def _mm_exact_lhs(a_bf16, b, n):
    acc = None
    for h in _split(b, n):
        t = jnp.dot(a_bf16, h, preferred_element_type=F32)
        acc = t if acc is None else acc + t
    return acc


def _mm3(a, b):
    ah, al = _split(a, 2)
    bh, bl = _split(b, 2)
    return (jnp.dot(ah, bh, preferred_element_type=F32) + jnp.dot(al, bh, preferred_element_type=F32)
            + jnp.dot(ah, bl, preferred_element_type=F32))


def _ln(x, g, b, eps):
    mu = jnp.mean(x, -1, keepdims=True)
    xc = x - mu
    var = jnp.mean(xc * xc, -1, keepdims=True)
    return xc * lax.rsqrt(var + eps) * g + b


def _row(a):
    return a.reshape(1, -1)


def _mod_kernel(c_ref, w_ref, b_ref, o_ref):
    c = c_ref[...]
    o_ref[...] = _mm3(c * _sigmoid(c), w_ref[...]) + b_ref[...]


def _mod_call(c, w_mod, b_mod):
    n, d = c.shape
    npad = -(-n // 8) * 8
    cp = jnp.pad(c, ((0, npad - n), (0, 0)))
    tn = 1536
    out = pl.pallas_call(
        _mod_kernel,
        grid=(6 * d // tn,),
        in_specs=[pl.BlockSpec((npad, d), lambda j: (0, 0)),
                  pl.BlockSpec((d, tn), lambda j: (0, j)),
                  pl.BlockSpec((1, tn), lambda j: (0, j))],
        out_specs=pl.BlockSpec((npad, tn), lambda j: (0, j)),
        out_shape=jax.ShapeDtypeStruct((npad, 6 * d), F32),
        compiler_params=_params(("arbitrary",)),
        name="mod",
    )(cp, w_mod, _row(b_mod))
    return out[:n].reshape(n, 6, d)


def _inproj_kernel(x_ref, mod_ref, g_ref, b_ref, wr_ref, wc_ref, pr_ref, pc_ref):
    x0 = _ln(x_ref[0], g_ref[...], b_ref[...], LN_EPS)
    m = mod_ref[0]
    h = (x0 * (1.0 + m[1:2]) + m[0:1]).astype(BF16)
    pr_ref[0] = jnp.dot(h, wr_ref[...], preferred_element_type=F32).astype(pr_ref.dtype)
    pc_ref[0] = jnp.dot(h, wc_ref[...], preferred_element_type=F32).astype(pc_ref.dtype)


def _inproj_call(x, mod, in_g, in_b, w_r, w_c):
    B, T, D = x.shape
    tm = min(512, T)
    const = lambda b, t: (0, 0)
    return pl.pallas_call(
        _inproj_kernel,
        grid=(B, T // tm),
        in_specs=[pl.BlockSpec((1, tm, D), lambda b, t: (b, t, 0)),
                  pl.BlockSpec((1, 6, D), lambda b, t: (b, 0, 0)),
                  pl.BlockSpec((1, D), const), pl.BlockSpec((1, D), const),
                  pl.BlockSpec(w_r.shape, const), pl.BlockSpec(w_c.shape, const)],
        out_specs=[pl.BlockSpec((1, tm, N_SHIFT_COLS), lambda b, t: (b, t, 0)),
                   pl.BlockSpec((1, tm, 2 * D_CONV), lambda b, t: (b, t, 0))],
        out_shape=[jax.ShapeDtypeStruct((B, T, N_SHIFT_COLS), BF16),
                   jax.ShapeDtypeStruct((B, T, 2 * D_CONV), BF16)],
        compiler_params=_params(("parallel", "parallel")),
        name="inproj",
    )(x, mod, _row(in_g), _row(in_b), w_r, w_c)


def _pre_kernel(p_ref, pp_ref, pn_ref, mu_ref, w0_ref, w2_ref, a0_ref, a2_ref, gup_ref, kkw_ref, ka_ref,
                rk_ref, bd_ref, r_o, v_o, kk_o, lw_o, kd_o, bb_o, g_o, bonus_o):
    t = pl.program_id(1)
    nt = pl.num_programs(1)
    p = p_ref[0].astype(F32)
    tm = p.shape[0]
    prev_row = jnp.where(t > 0, pp_ref[0].astype(F32)[HALO - 1:HALO], 0.0)
    next_row = jnp.where(t < nt - 1, pn_ref[0].astype(F32)[0:1], 0.0)
    row = lax.broadcasted_iota(jnp.int32, p.shape, 0)
    prev = jnp.where(row == 0, prev_row, pltpu.roll(p, 1, 0))
    nxt = jnp.where(row == tm - 1, next_row, pltpu.roll(p, tm - 1, 0))
    mu = mu_ref[...]
    ps = p + mu[0:1] * (prev - p) + mu[1:2] * (nxt - p)

    r = ps[:, 0:D_RWKV]
    k = ps[:, D_RWKV:2 * D_RWKV]
    v = ps[:, 2 * D_RWKV:3 * D_RWKV]
    o = 3 * D_RWKV
    wl = ps[:, o:o + 2 * DECAY_RANK]
    al = ps[:, o + 2 * DECAY_RANK:o + 2 * DECAY_RANK + 2 * AAA_RANK]
    gl = ps[:, o + 2 * DECAY_RANK + 2 * AAA_RANK:]

    y = -(w0_ref[...] + _mm(jnp.tanh(wl), w2_ref[...]))
    softplus = jnp.maximum(y, 0.0) + jnp.log(1.0 + jnp.exp(-jnp.abs(y)))
    lw = -jnp.exp(-softplus - 0.5)
    a = _sigmoid(a0_ref[...] + _mm(al, a2_ref[...]))
    g = _mm(_sigmoid(gl), gup_ref[...])

    bd = bd_ref[...]
    kk = k * kkw_ref[...]
    ss = _mm_exact_rhs(kk * kk, bd, 2)
    kk = kk * lax.rsqrt(jnp.maximum(ss, 1e-24))
    ka = ka_ref[...]
    a0d, a1d = a[:, :D_RWKV], a[:, D_RWKV:]
    k0 = k * (1.0 + (a0d - 1.0) * ka)
    k1 = k * (1.0 + (a1d - 1.0) * ka)
    bonus = _mm_exact_rhs(r * (k0 + k1) * rk_ref[...], bd, 2) * v

    r_o[0] = r
    v_o[0] = v
    kk_o[0] = kk
    lw_o[0] = lw
    kd_o[0, :, :D_RWKV] = k0
    kd_o[0, :, D_RWKV:] = k1
    bb_o[0, :, :D_RWKV] = kk * a0d
    bb_o[0, :, D_RWKV:] = kk * a1d
    g_o[0] = g
    bonus_o[0] = bonus


def _pre_call(p_r, mu_shift, w0c, w2c, a0c, a2c, g_up, k_k, k_a, r_k, bd):
    B, T, C = p_r.shape
    tm = min(256, T)
    nh = tm // HALO
    const = lambda b, t: (0, 0)
    tile = lambda w: pl.BlockSpec((1, tm, w), lambda b, t: (b, t, 0))
    outs = [D_RWKV, D_RWKV, D_RWKV, 2 * D_RWKV, 2 * D_RWKV, 2 * D_RWKV, D_RWKV, D_RWKV]
    return pl.pallas_call(
        _pre_kernel,
        grid=(B, T // tm),
        in_specs=[tile(C),
                  pl.BlockSpec((1, HALO, C), lambda b, t: (b, jnp.maximum(t * nh - 1, 0), 0)),
                  pl.BlockSpec((1, HALO, C), lambda b, t: (b, jnp.minimum((t + 1) * nh, T // HALO - 1), 0)),
                  pl.BlockSpec(mu_shift.shape, const), pl.BlockSpec(w0c.shape, const),
                  pl.BlockSpec(w2c.shape, const), pl.BlockSpec(a0c.shape, const),
                  pl.BlockSpec(a2c.shape, const), pl.BlockSpec(g_up.shape, const),
                  pl.BlockSpec(k_k.shape, const), pl.BlockSpec(k_a.shape, const),
                  pl.BlockSpec(r_k.shape, const), pl.BlockSpec(bd.shape, const)],
        out_specs=[tile(w) for w in outs],
        out_shape=[jax.ShapeDtypeStruct((B, T, w), F32) for w in outs],
        compiler_params=_params(("parallel", "parallel")),
        name="pre",
    )(p_r, p_r, p_r, mu_shift, w0c, w2c, a0c, a2c, g_up, k_k, k_a, r_k, bd)


def _chains_chunk(chains):
    assert CHUNK == HEAD_DIM
    L = chains[0]["rt"].shape[0]
    R = GROUP * L
    rowh = lax.broadcasted_iota(jnp.int32, (R, GROUP_W), 0) // L
    colh = lax.broadcasted_iota(jnp.int32, (R, GROUP_W), 1) // HEAD_DIM
    own = rowh == colh

    def bd(x):
        return jnp.where(own, jnp.concatenate([x.astype(BF16)] * GROUP, axis=0), jnp.zeros((), BF16))

    def dot(a, b):
        return jnp.dot(a.astype(BF16), b, preferred_element_type=F32)

    t_row = lax.broadcasted_iota(jnp.int32, (L, GROUP_W), 0)
    i_col = lax.broadcasted_iota(jnp.int32, (L, GROUP_W), 1) % L
    eye = jnp.where(t_row == i_col, 1.0, 0.0)
    masks = {False: (i_col < t_row, i_col <= t_row), True: (i_col > t_row, i_col >= t_row)}

    zr = [jnp.concatenate([c["zt"], c["rt"]], axis=0).astype(BF16) for c in chains]
    a_b = [_mm_nt(x, bd(c["bt"])) for x, c in zip(zr, chains)]
    a_k = [_mm_nt(x, bd(c["kt"])) for x, c in zip(zr, chains)]
    n_zb = [jnp.where(masks[c["reverse"]][0], a[:L], 0.0) for c, a in zip(chains, a_b)]
    a_rb = [jnp.where(masks[c["reverse"]][1], a[L:], 0.0) for c, a in zip(chains, a_b)]
    a_zr = [jnp.concatenate([jnp.where(masks[c["reverse"]][0], a[:L], 0.0),
                             jnp.where(masks[c["reverse"]][1], a[L:], 0.0)], axis=0) for c, a in zip(chains, a_k)]

    tinv = [eye + n for n in n_zb]
    pw = [dot(n, bd(n)) for n in n_zb]
    levels = L.bit_length() - 2
    for j in range(1, levels + 1):
        if j < levels:
            both = [dot(jnp.concatenate([p, t], axis=0), bd(p)) for p, t in zip(pw, tinv)]
            pw = [b[:L] for b in both]
            tinv = [t + b[L:] for t, b in zip(tinv, both)]
        else:
            tinv = [t + dot(t, bd(p)) for p, t in zip(pw, tinv)]

    av = [dot(a, bd(c["v"])) for a, c in zip(a_zr, chains)]

    def rows(x):
        return jnp.concatenate([x[:, h * HEAD_DIM:(h + 1) * HEAD_DIM] for h in range(GROUP)], axis=0)

    def advance(idx, S):
        zs = [_mm_nt(zr[i], bd(s)) for i, s in zip(idx, S)]
        u = [dot(tinv[i], bd(z[:L] + av[i][:L])) for i, z in zip(idx, zs)]
        outs = [z[L:] + av[i][L:] + dot(a_rb[i], bd(x)) for i, z, x in zip(idx, zs, u)]
        nxt = [s * chains[i]["wtot"] + _mm_tn(rows(x), bd(chains[i]["bhat"]))
               + _mm_tn(rows(chains[i]["v"]), bd(chains[i]["khat"])) for i, s, x in zip(idx, S, u)]
        return outs, nxt

    return advance


def _prep_chains(r, v, kk, lw, kd, bb, reverse):
    L = r.shape[0]
    ti = lax.broadcasted_iota(jnp.int32, (L, L), 0)
    ii = lax.broadcasted_iota(jnp.int32, (L, L), 1)
    tri = jnp.where((ii >= ti) if reverse else (ii <= ti), 1.0, 0.0).astype(BF16)
    cum = _mm_exact_lhs(tri, lw, 3)
    tot = cum[0:1] if reverse else cum[L - 1:L]
    ec, eci, ecx, eto = jnp.exp(cum), jnp.exp(-cum), jnp.exp(cum - lw), jnp.exp(tot - cum)
    full = dict(rt=r * ec, zt=-(kk * ecx), kt=kd * eci, bt=bb * eci, khat=kd * eto, bhat=bb * eto, v=v,
                wtot=jnp.exp(tot))
    chains = []
    for g in range(D_RWKV // GROUP_W):
        c = {k: x[:, g * GROUP_W:(g + 1) * GROUP_W] for k, x in full.items()}
        c["reverse"] = reverse
        chains.append(c)
    return chains


def _scan_kernel(rf, vf, kkf, lwf, kdf, bbf, rb, vb, kkb, lwb, kdb, bbb, of, ob, s_ref):
    @pl.when(pl.program_id(1) == 0)
    def _():
        s_ref[...] = jnp.zeros_like(s_ref)

    n_sub = rf.shape[1] // CHUNK
    n_grp = D_RWKV // GROUP_W
    per = 2 * n_grp
    chains, spans = [], []
    for i in range(n_sub):
        lf = pl.ds(i * CHUNK, CHUNK)
        lb = pl.ds((n_sub - 1 - i) * CHUNK, CHUNK)
        spans.append((lf, lb))
        chains += _prep_chains(rf[0, lf], vf[0, lf], kkf[0, lf], lwf[0, lf], kdf[0, lf], bbf[0, lf], False)
        chains += _prep_chains(rb[0, lb], vb[0, lb], kkb[0, lb], lwb[0, lb], kdb[0, lb], bbb[0, lb], True)
    advance = _chains_chunk(chains)
    sf, sb = s_ref[0], s_ref[1]
    states = ([sf[:, g * GROUP_W:(g + 1) * GROUP_W] for g in range(n_grp)]
              + [sb[:, g * GROUP_W:(g + 1) * GROUP_W] for g in range(n_grp)])
    for i, (lf, lb) in enumerate(spans):
        outs, states = advance(list(range(i * per, (i + 1) * per)), states)
        of[0, lf] = jnp.concatenate(outs[:n_grp], axis=1)
        ob[0, lb] = jnp.concatenate(outs[n_grp:], axis=1)
    s_ref[0] = jnp.concatenate(states[:n_grp], axis=1)
    s_ref[1] = jnp.concatenate(states[n_grp:], axis=1)


def _scan_call(r, v, kk, lw, kd, bb):
    B, T, C = r.shape
    L = min(SCAN_BLOCK, T)
    nc = T // L
    fwd = lambda j: pl.BlockSpec((1, L, C), lambda b, c: (b, c, j))
    bwd = lambda j: pl.BlockSpec((1, L, C), lambda b, c: (b, nc - 1 - c, j))
    return pl.pallas_call(
        _scan_kernel,
        grid=(B, nc),
        in_specs=[fwd(0)] * 6 + [bwd(0)] * 3 + [bwd(1)] * 3,
        out_specs=[fwd(0), bwd(0)],
        out_shape=[jax.ShapeDtypeStruct((B, T, C), F32)] * 2,
        scratch_shapes=[pltpu.VMEM((2, HEAD_DIM, C), F32)],
        compiler_params=_params(("parallel", "arbitrary")),
        name="scan",
    )(r, v, kk, lw, kd, bb, r, v, kk, lw, kd, bb)


def _conv_kernel(p_ref, pp_ref, pn_ref, w_ref, b_ref, g_ref, be_ref, o_ref, h_ref):
    t = pl.program_id(1)
    nt = pl.num_programs(1)
    tm = p_ref.shape[1]

    def glu(x):
        x = x.astype(F32)
        return x[:, :D_CONV] * _sigmoid(x[:, D_CONV:])

    h_ref[0:HALO] = jnp.where(t > 0, glu(pp_ref[0]), 0.0)
    h_ref[HALO:HALO + tm] = glu(p_ref[0])
    h_ref[HALO + tm:] = jnp.where(t < nt - 1, glu(pn_ref[0]), 0.0)
    w = w_ref[...]
    acc = jnp.zeros((tm, D_CONV), F32) + b_ref[...]
    base = HALO - CONV_WIDTH // 2
    for j in range(CONV_WIDTH):
        acc = acc + w[j:j + 1] * h_ref[base + j:base + j + tm]
    y = _ln(acc, g_ref[...], be_ref[...], LN_EPS)
    o_ref[0] = (y * _sigmoid(y)).astype(o_ref.dtype)


def _conv_call(p_c, dw_w, dw_b, cln_g, cln_b):
    B, T, C = p_c.shape
    tm = min(256, T)
    nh = tm // HALO
    const = lambda b, t: (0, 0)
    return pl.pallas_call(
        _conv_kernel,
        grid=(B, T // tm),
        in_specs=[pl.BlockSpec((1, tm, C), lambda b, t: (b, t, 0)),
                  pl.BlockSpec((1, HALO, C), lambda b, t: (b, jnp.maximum(t * nh - 1, 0), 0)),
                  pl.BlockSpec((1, HALO, C), lambda b, t: (b, jnp.minimum((t + 1) * nh, T // HALO - 1), 0)),
                  pl.BlockSpec(dw_w.shape, const), pl.BlockSpec((1, D_CONV), const),
                  pl.BlockSpec((1, D_CONV), const), pl.BlockSpec((1, D_CONV), const)],
        out_specs=pl.BlockSpec((1, tm, D_CONV), lambda b, t: (b, t, 0)),
        out_shape=jax.ShapeDtypeStruct((B, T, D_CONV), BF16),
        scratch_shapes=[pltpu.VMEM((tm + 2 * HALO, D_CONV), F32)],
        compiler_params=_params(("parallel", "parallel")),
        name="conv",
    )(p_c, p_c, p_c, dw_w, _row(dw_b), _row(cln_g), _row(cln_b))


def _post_kernel(x_ref, mod_ref, ing_ref, inb_ref, wf_ref, wb_ref, bonus_ref, g_ref, yb_ref, gng_ref, gnb_ref,
                 bd_ref, wo_ref, l1g_ref, l1b_ref, wr_ref, br_ref, x1_o, h2_o, gate_o):
    m = mod_ref[0]
    x0 = _ln(x_ref[0], ing_ref[...], inb_ref[...], LN_EPS)
    bd = bd_ref[...]
    wkv = wf_ref[0] + wb_ref[0]
    mu = _mm_exact_rhs(wkv, bd, 3) * (1.0 / HEAD_DIM)
    wc = wkv - mu
    var = _mm_exact_rhs(wc * wc, bd, 2) * (1.0 / HEAD_DIM)
    o = wc * lax.rsqrt(var + GN_EPS) * gng_ref[...] + gnb_ref[...]
    ya = (o + bonus_ref[0]) * g_ref[0]
    wo = wo_ref[...]
    mix = _mm(ya, wo[:D_RWKV]) + jnp.dot(yb_ref[0], wo[D_RWKV:], preferred_element_type=F32)
    x1 = _ln(DEEPNORM_ALPHA * x0 + (1.0 + m[2:3]) * mix, l1g_ref[...], l1b_ref[...], LN_EPS)
    x1_o[0] = x1
    h2 = x1 * (1.0 + m[4:5]) + m[3:4]
    h2_o[0] = h2.astype(h2_o.dtype)

    logits = _mm3(h2, wr_ref[...]) + br_ref[...]
    lane = lax.broadcasted_iota(jnp.int32, logits.shape, 1)
    work = logits
    sel = jnp.zeros(logits.shape, F32)
    top = None
    for _ in range(TOP_K):
        mx = jnp.max(work, -1, keepdims=True)
        first = jnp.min(jnp.where(work == mx, lane, LANES), -1, keepdims=True)
        pick = lane == first
        sel = jnp.where(pick, 1.0, sel)
        work = jnp.where(pick, -jnp.inf, work)
        top = mx if top is None else top
    e = jnp.where(sel > 0.0, jnp.exp(logits - top), 0.0)
    gates = e / jnp.sum(e, -1, keepdims=True)
    gate_o[0] = gates.T[:N_EXPERTS]


def _post_call(x, mod, in_g, in_b, wkv_f, wkv_b, bonus, g, y_b, gn_g, gn_b, bd, w_out, ln1_g, ln1_b,
               w_router_p, b_router_p):
    B, T, D = x.shape
    tm = min(256, T)
    const = lambda b, t: (0, 0)
    tile = lambda w: pl.BlockSpec((1, tm, w), lambda b, t: (b, t, 0))
    vec = lambda w: pl.BlockSpec((1, w), const)
    return pl.pallas_call(
        _post_kernel,
        grid=(B, T // tm),
        in_specs=[tile(D), pl.BlockSpec((1, 6, D), lambda b, t: (b, 0, 0)), vec(D), vec(D),
                  tile(D_RWKV), tile(D_RWKV), tile(D_RWKV), tile(D_RWKV), tile(D_CONV),
                  vec(D_RWKV), vec(D_RWKV), pl.BlockSpec(bd.shape, const),
                  pl.BlockSpec(w_out.shape, const), vec(D), vec(D),
                  pl.BlockSpec(w_router_p.shape, const), vec(LANES)],
        out_specs=[tile(D), tile(D), pl.BlockSpec((1, N_EXPERTS, tm), lambda b, t: (b, 0, t))],
        out_shape=[jax.ShapeDtypeStruct((B, T, D), F32), jax.ShapeDtypeStruct((B, T, D), BF16),
                   jax.ShapeDtypeStruct((B, N_EXPERTS, T), F32)],
        compiler_params=_params(("parallel", "parallel")),
        name="post",
    )(x, mod, _row(in_g), _row(in_b), wkv_f, wkv_b, bonus, g, y_b, _row(gn_g), _row(gn_b), bd, w_out,
      _row(ln1_g), _row(ln1_b), w_router_p, b_router_p)


def _routing(gates_t, tb):
    B, E, T = gates_t.shape
    nb = T // tb
    n_blocks = B * nb
    cnt = jnp.sum((gates_t > 0.0).reshape(B, E, nb, tb), -1, dtype=jnp.int32)
    cnt = cnt.transpose(1, 0, 2).reshape(E, n_blocks)
    cnt_al = (cnt + ROW_ALIGN - 1) // ROW_ALIGN * ROW_ALIGN
    used = jnp.sum(cnt_al, 1) + CAP
    seg = (used + FFN_TILE - 1) // FFN_TILE * FFN_TILE
    seg_end = jnp.cumsum(seg)
    seg_start = seg_end - seg
    start = seg_start[:, None] + jnp.cumsum(cnt_al, 1) - cnt_al
    rows_max = TOP_K * B * T + (ROW_ALIGN - 1) * n_blocks * E + E * (CAP + FFN_TILE - 1)
    n_tiles = -(-rows_max // FFN_TILE)
    idx = jnp.arange(n_tiles, dtype=jnp.int32)
    n_valid = seg_end[-1] // FFN_TILE
    src = jnp.minimum(idx, n_valid - 1)
    tile_e = jnp.sum(seg_end[None, :] <= (src * FFN_TILE)[:, None], 1, dtype=jnp.int32)
    over = jnp.max(cnt, 0) > CAP
    return dict(start=start.reshape(-1), cnt=cnt.reshape(-1), over=over.astype(jnp.int32),
                tile_e=tile_e, src=src, valid=(idx < n_valid).astype(jnp.int32), n_tiles=n_tiles)


def _block_ranks(g, tri):
    routed = g > 0.0
    rank = jnp.dot(jnp.where(routed, 1.0, 0.0).astype(BF16), tri, preferred_element_type=F32)
    return jnp.where(routed, rank, -1.0)


def _group_copy(buf, hbm, sem, slot, row):
    return pltpu.make_async_copy(buf.at[pl.ds(slot * CAP, CAP)],
                                 hbm.at[pl.ds(pl.multiple_of(row, ROW_ALIGN), CAP)], sem.at[0])


def _dispatch_kernel(start_ref, cnt_ref, over_ref, h_ref, g_ref, tri_ref, xs_in, xs_out, stage, rank_s, sem):
    del xs_in
    E, tb = g_ref.shape[1], g_ref.shape[2]
    n_blocks = pl.num_programs(0) * pl.num_programs(1)
    blk = pl.program_id(0) * pl.num_programs(1) + pl.program_id(1)
    rank = _block_ranks(g_ref[0], tri_ref[...])
    rank_s[...] = rank
    slot = lax.broadcasted_iota(jnp.int32, (CAP, tb), 0).astype(F32)
    h = h_ref[0]
    n_part = 4
    per = E // n_part
    for q in range(n_part):
        sel = jnp.concatenate([jnp.where(rank[e:e + 1] == slot, 1.0, 0.0).astype(BF16)
                               for e in range(q * per, (q + 1) * per)], axis=0)
        stage[q * per * CAP:(q + 1) * per * CAP] = jnp.dot(sel, h, preferred_element_type=F32).astype(BF16)
        for e in range(q * per, (q + 1) * per):
            _group_copy(stage, xs_out, sem, e, start_ref[e * n_blocks + blk]).start()
    for e in range(E):
        _group_copy(stage, xs_out, sem, e, start_ref[e * n_blocks + blk]).wait()

    @pl.when(over_ref[blk] > 0)
    def _():
        def per_expert(e, carry):
            def per_chunk(c, carry):
                row = rank_s[pl.ds(e, 1), :] - (c * CAP).astype(F32)
                sel = jnp.where(row == slot, 1.0, 0.0).astype(BF16)
                stage[0:CAP] = jnp.dot(sel, h, preferred_element_type=F32).astype(BF16)
                cp = _group_copy(stage, xs_out, sem, 0, start_ref[e * n_blocks + blk] + c * CAP)
                cp.start()
                cp.wait()
                return carry
            n_chunks = (cnt_ref[e * n_blocks + blk] + CAP - 1) // CAP
            return lax.fori_loop(1, n_chunks, per_chunk, carry)
        lax.fori_loop(0, E, per_expert, 0)


def _dispatch_call(h2, gates_t, tri, rt):
    B, T, D = h2.shape
    E = gates_t.shape[1]
    tb = tri.shape[0]
    rows = rt["n_tiles"] * FFN_TILE
    xs0 = jnp.zeros((rows, D), BF16)
    grid_spec = pltpu.PrefetchScalarGridSpec(
        num_scalar_prefetch=3,
        grid=(B, T // tb),
        in_specs=[pl.BlockSpec((1, tb, D), lambda b, t, *_: (b, t, 0)),
                  pl.BlockSpec((1, E, tb), lambda b, t, *_: (b, 0, t)),
                  pl.BlockSpec((tb, tb), lambda b, t, *_: (0, 0)),
                  pl.BlockSpec(memory_space=pl.ANY)],
        out_specs=pl.BlockSpec(memory_space=pl.ANY),
        scratch_shapes=[pltpu.VMEM((E * CAP, D), BF16), pltpu.VMEM((E, tb), F32), pltpu.SemaphoreType.DMA((1,))],
    )
    return pl.pallas_call(
        _dispatch_kernel, grid_spec=grid_spec,
        out_shape=jax.ShapeDtypeStruct((rows, D), BF16),
        input_output_aliases={6: 0},
        compiler_params=_params(("arbitrary", "arbitrary")),
        name="dispatch",
    )(rt["start"], rt["cnt"], rt["over"], h2, gates_t, tri, xs0)


def _ffn_kernel(te_ref, src_ref, valid_ref, x_ref, wgu_ref, bgu_ref, wd_ref, bdn_ref, o_ref):
    i = pl.program_id(0)

    @pl.when(valid_ref[i] > 0)
    def _():
        h = jnp.dot(x_ref[...], wgu_ref[0], preferred_element_type=F32) + bgu_ref[0]
        h_glu = jnp.minimum(h[:, :D_FF], SWIGLU_LIMIT)
        h_lin = jnp.clip(h[:, D_FF:], -SWIGLU_LIMIT, SWIGLU_LIMIT)
        y = (h_lin + 1.0) * (h_glu * _sigmoid(SWIGLU_ALPHA * h_glu))
        o_ref[...] = (_mm(y, wd_ref[0]) + bdn_ref[0]).astype(o_ref.dtype)

    @pl.when(valid_ref[i] == 0)
    def _():
        o_ref[...] = jnp.zeros_like(o_ref)


def _ffn_call(xs, rt, w_gu, b_gu, w_down, b_down):
    rows, D = xs.shape
    E = w_gu.shape[0]
    grid_spec = pltpu.PrefetchScalarGridSpec(
        num_scalar_prefetch=3,
        grid=(rt["n_tiles"],),
        in_specs=[pl.BlockSpec((FFN_TILE, D), lambda i, te, src, valid: (src[i], 0)),
                  pl.BlockSpec((1, D, 2 * D_FF), lambda i, te, src, valid: (te[i], 0, 0)),
                  pl.BlockSpec((1, 1, 2 * D_FF), lambda i, te, src, valid: (te[i], 0, 0)),
                  pl.BlockSpec((1, D_FF, D), lambda i, te, src, valid: (te[i], 0, 0)),
                  pl.BlockSpec((1, 1, D), lambda i, te, src, valid: (te[i], 0, 0))],
        out_specs=pl.BlockSpec((FFN_TILE, D), lambda i, te, src, valid: (i, 0)),
    )
    return pl.pallas_call(
        _ffn_kernel, grid_spec=grid_spec,
        out_shape=jax.ShapeDtypeStruct((rows, D), BF16),
        compiler_params=_params(("arbitrary",)),
        name="ffn",
    )(rt["tile_e"], rt["src"], rt["valid"], xs, w_gu, b_gu.reshape(E, 1, -1), w_down,
      b_down.reshape(E, 1, -1))


def _combine_kernel(start_ref, cnt_ref, over_ref, g_ref, x1_ref, mod_ref, tri_ref, lg_ref, lb_ref, ys_ref, o_ref,
                    ybuf, xbuf, rank_s, acc_s, sem):
    E, tb = g_ref.shape[1], g_ref.shape[2]
    n_blocks = pl.num_programs(0) * pl.num_programs(1)
    blk = pl.program_id(0) * pl.num_programs(1) + pl.program_id(1)
    cur = blk % 2

    def fetch(half, e, block):
        row = start_ref[e * n_blocks + block]
        return pltpu.make_async_copy(ys_ref.at[pl.ds(pl.multiple_of(row, ROW_ALIGN), CAP)],
                                     ybuf.at[half, pl.ds(e * CAP, CAP)], sem.at[half])

    @pl.when(blk == 0)
    def _():
        for e in range(E):
            fetch(0, e, 0).start()

    @pl.when(blk + 1 < n_blocks)
    def _():
        for e in range(E):
            fetch(1 - cur, e, blk + 1).start()

    g = g_ref[0]
    rank = _block_ranks(g, tri_ref[...])
    slot = lax.broadcasted_iota(jnp.int32, (CAP, tb), 0).astype(F32)
    for e in range(E):
        fetch(cur, e, blk).wait()
    n_part = 4
    per = E // n_part
    acc = None
    for q in range(n_part):
        w = jnp.concatenate([jnp.where(rank[e:e + 1] == slot, g[e:e + 1], 0.0).astype(BF16)
                             for e in range(q * per, (q + 1) * per)], axis=0)
        part = _mm_tn(w, ybuf[cur, q * per * CAP:(q + 1) * per * CAP])
        acc = part if acc is None else acc + part
    acc_s[...] = acc

    @pl.when(over_ref[blk] > 0)
    def _():
        rank_s[...] = rank

        def per_expert(e, carry):
            def per_chunk(c, carry):
                row0 = start_ref[e * n_blocks + blk] + c * CAP
                cp = pltpu.make_async_copy(ys_ref.at[pl.ds(pl.multiple_of(row0, ROW_ALIGN), CAP)], xbuf, sem.at[2])
                cp.start()
                row = rank_s[pl.ds(e, 1), :] - (c * CAP).astype(F32)
                w = jnp.where(row == slot, g_ref[0, pl.ds(e, 1), :], 0.0).astype(BF16)
                cp.wait()
                acc_s[...] += _mm_tn(w, xbuf[...])
                return carry
            n_chunks = (cnt_ref[e * n_blocks + blk] + CAP - 1) // CAP
            return lax.fori_loop(1, n_chunks, per_chunk, carry)
        lax.fori_loop(0, E, per_expert, 0)

    m = mod_ref[0]
    o_ref[0] = _ln(DEEPNORM_ALPHA * x1_ref[0] + (1.0 + m[5:6]) * acc_s[...], lg_ref[...], lb_ref[...], LN_EPS)


def _combine_call(ys, gates_t, x1, mod, tri, rt, ln2_g, ln2_b):
    B, T, D = x1.shape
    E = gates_t.shape[1]
    tb = tri.shape[0]
    const = lambda b, t, *_: (0, 0)
    grid_spec = pltpu.PrefetchScalarGridSpec(
        num_scalar_prefetch=3,
        grid=(B, T // tb),
        in_specs=[pl.BlockSpec((1, E, tb), lambda b, t, *_: (b, 0, t)),
                  pl.BlockSpec((1, tb, D), lambda b, t, *_: (b, t, 0)),
                  pl.BlockSpec((1, 6, D), lambda b, t, *_: (b, 0, 0)),
                  pl.BlockSpec((tb, tb), const),
                  pl.BlockSpec((1, D), const), pl.BlockSpec((1, D), const),
                  pl.BlockSpec(memory_space=pl.ANY)],
        out_specs=pl.BlockSpec((1, tb, D), lambda b, t, *_: (b, t, 0)),
        scratch_shapes=[pltpu.VMEM((2, E * CAP, D), BF16), pltpu.VMEM((CAP, D), BF16), pltpu.VMEM((E, tb), F32),
                        pltpu.VMEM((tb, D), F32), pltpu.SemaphoreType.DMA((3,))],
    )
    return pl.pallas_call(
        _combine_kernel, grid_spec=grid_spec,
        out_shape=jax.ShapeDtypeStruct((B, T, D), F32),
        compiler_params=_params(("arbitrary", "arbitrary")),
        name="combine",
    )(rt["start"], rt["cnt"], rt["over"], gates_t, x1, mod, tri, _row(ln2_g), _row(ln2_b), ys)


def _moe_call(h2, gates_t, x1, mod, w_gu, b_gu, w_down, b_down, ln2_g, ln2_b):
    tb = min(TOKEN_BLOCK, h2.shape[1])
    pos = lax.broadcasted_iota(jnp.int32, (tb, tb), 0)
    tri = (pos < pos.T).astype(BF16)
    rt = _routing(gates_t, tb)
    xs = _dispatch_call(h2, gates_t, tri, rt)
    ys = _ffn_call(xs, rt, w_gu, b_gu, w_down, b_down)
    return _combine_call(ys, gates_t, x1, mod, tri, rt, ln2_g, ln2_b)


def _block_diag_cat(w):
    z = jnp.zeros_like(w[0])
    return jnp.concatenate([jnp.concatenate([w[0], z], 1), jnp.concatenate([z, w[1]], 1)], 0)


def _encode(x, mod, W):
    p_r, p_c = _inproj_call(x, mod, W["in_g"], W["in_b"], W["w_in_r"], W["w_in_c"])
    r, v, kk, lw, kd, bb, g, bonus = _pre_call(p_r, W["mu_shift"], W["w0c"], W["w2c"], W["a0c"], W["a2c"],
                                               W["g_up"], W["k_k"], W["k_a"], W["r_k"], W["bd"])
    wkv_f, wkv_b = _scan_call(r, v, kk, lw, kd, bb)
    y_b = _conv_call(p_c, W["dw_w"], W["dw_b"], W["cln_g"], W["cln_b"])
    x1, h2, gates = _post_call(x, mod, W["in_g"], W["in_b"], wkv_f, wkv_b, bonus, g, y_b, W["gn_g"], W["gn_b"],
                               W["bd"], W["w_out"], W["ln1_g"], W["ln1_b"], W["w_router_p"], W["b_router_p"])
    return _moe_call(h2, gates, x1, mod, W["w_gu"], W["b_gu"], W["w_down"], W["b_down"], W["ln2_g"], W["ln2_b"])


def kernel(x_prompt, x_sample, c_prompt, c_sample, in_g, in_b, w_mod, b_mod, w_in, mu_shift, w0, w2, a0, a2, g_up, k_k, k_a, r_k, gn_g, gn_b, dw_w, dw_b, cln_g, cln_b, w_out, ln1_g, ln1_b, w_router, b_router, w_gu, b_gu, w_down, b_down, ln2_g, ln2_b):
    l = 0
    head = lax.broadcasted_iota(jnp.int32, (D_RWKV, D_RWKV), 0) // HEAD_DIM
    W = dict(
        in_g=in_g, in_b=in_b,
        w_in_r=w_in[l][:, :N_SHIFT_COLS].astype(BF16), w_in_c=w_in[l][:, N_SHIFT_COLS:].astype(BF16),
        mu_shift=mu_shift[l],
        w0c=w0[l].reshape(1, -1), w2c=_block_diag_cat(w2[l]).astype(BF16),
        a0c=a0[l].reshape(1, -1), a2c=_block_diag_cat(a2[l]).astype(BF16),
        g_up=g_up[l].astype(BF16), k_k=_row(k_k[l]), k_a=_row(k_a[l]), r_k=_row(r_k[l]),
        bd=(head == head.T).astype(BF16),
        gn_g=gn_g[l], gn_b=gn_b[l], dw_w=dw_w[l], dw_b=dw_b[l], cln_g=cln_g[l], cln_b=cln_b[l],
        w_out=w_out[l].astype(BF16), ln1_g=ln1_g[l], ln1_b=ln1_b[l],
        w_router_p=jnp.pad(w_router[l], ((0, 0), (0, LANES - N_EXPERTS))),
        b_router_p=jnp.pad(b_router[l], (0, LANES - N_EXPERTS), constant_values=-1e30).reshape(1, -1),
        w_gu=w_gu[l].astype(BF16), b_gu=b_gu[l], w_down=w_down[l].astype(BF16), b_down=b_down[l],
        ln2_g=ln2_g[l], ln2_b=ln2_b[l],
    )
    nb = x_prompt.shape[0]
    mod = _mod_call(jnp.concatenate([c_prompt, c_sample], 0), w_mod[l], b_mod[l])
    return (_encode(x_prompt, mod[:nb], W), _encode(x_sample, mod[nb:], W))
```

```python
import functools

import jax
import jax.numpy as jnp
from jax import lax
from jax.experimental import pallas as pl
from jax.experimental.pallas import tpu as pltpu

F32 = jnp.float32
BF16 = jnp.bfloat16

D_MODEL = 1024
D_RWKV = 512
HEAD_DIM = 64
N_HEADS = 8
D_CONV = 512
DECAY_RANK = 64
AAA_RANK = 64
GATE_RANK = 128
CONV_WIDTH = 31
N_EXPERTS = 32
TOP_K = 4
D_FF = 1024
SWIGLU_LIMIT = 7.0
SWIGLU_ALPHA = 1.702
LN_EPS = 1e-5
GN_EPS = 64e-5
DEPTH = 1
DEEPNORM_ALPHA = (2.0 * DEPTH) ** 0.25
N_SHIFT_COLS = 3 * D_RWKV + 2 * DECAY_RANK + 2 * AAA_RANK + GATE_RANK

LANES = 128
HALO = 16
CHUNK = 64
SCAN_BLOCK = 8 * CHUNK
TOKEN_BLOCK = 512
CAP = 128
ROW_ALIGN = 16
FFN_TILE = 512
GROUP = 4
GROUP_W = GROUP * HEAD_DIM
VMEM_LIMIT = 56 * 1024 * 1024


def _params(sem, flags=None):
    return pltpu.CompilerParams(dimension_semantics=sem, vmem_limit_bytes=VMEM_LIMIT, flags=flags)


def _sigmoid(x):
    return 1.0 / (1.0 + jnp.exp(-x))


def _mm(a, b):
    return jnp.dot(a.astype(BF16), b.astype(BF16), preferred_element_type=F32)


def _mm_nt(a, b):
    return lax.dot_general(a.astype(BF16), b.astype(BF16), (((1,), (1,)), ((), ())),
                           preferred_element_type=F32)


def _mm_tn(a, b):
    return lax.dot_general(a.astype(BF16), b.astype(BF16), (((0,), (0,)), ((), ())),
                           preferred_element_type=F32)


def _split(a, n):
    parts = []
    for _ in range(n):
        h = a.astype(BF16)
        parts.append(h)
        a = a - h.astype(F32)
    return parts


def _mm_exact_rhs(a, b_bf16, n):
    acc = None
    for h in _split(a, n):
        t = jnp.dot(h, b_bf16, preferred_element_type=F32)
        acc = t if acc is None else acc + t
    return acc


def _mm_exact_lhs(a_bf16, b, n):
    acc = None
    for h in _split(b, n):
        t = jnp.dot(a_bf16, h, preferred_element_type=F32)
        acc = t if acc is None else acc + t
    return acc


def _mm3(a, b):
    ah, al = _split(a, 2)
    bh, bl = _split(b, 2)
    return (jnp.dot(ah, bh, preferred_element_type=F32) + jnp.dot(al, bh, preferred_element_type=F32)
            + jnp.dot(ah, bl, preferred_element_type=F32))


def _ln(x, g, b, eps):
    mu = jnp.mean(x, -1, keepdims=True)
    xc = x - mu
    var = jnp.mean(xc * xc, -1, keepdims=True)
    return xc * lax.rsqrt(var + eps) * g + b


def _row(a):
    return a.reshape(1, -1)


def _mod_kernel(c_ref, w_ref, b_ref, o_ref):
    c = c_ref[...]
    o_ref[...] = _mm3(c * _sigmoid(c), w_ref[...]) + b_ref[...]


def _mod_call(c, w_mod, b_mod):
    n, d = c.shape
    npad = -(-n // 8) * 8
    cp = jnp.pad(c, ((0, npad - n), (0, 0)))
    tn = 1536
    out = pl.pallas_call(
        _mod_kernel,
        grid=(6 * d // tn,),
        in_specs=[pl.BlockSpec((npad, d), lambda j: (0, 0)),
                  pl.BlockSpec((d, tn), lambda j: (0, j)),
                  pl.BlockSpec((1, tn), lambda j: (0, j))],
        out_specs=pl.BlockSpec((npad, tn), lambda j: (0, j)),
        out_shape=jax.ShapeDtypeStruct((npad, 6 * d), F32),
        compiler_params=_params(("arbitrary",)),
        name="mod",
    )(cp, w_mod, _row(b_mod))
    return out[:n].reshape(n, 6, d)


def _inproj_kernel(x_ref, mod_ref, g_ref, b_ref, wr_ref, wc_ref, pr_ref, pc_ref):
    x0 = _ln(x_ref[0], g_ref[...], b_ref[...], LN_EPS)
    m = mod_ref[0]
    h = (x0 * (1.0 + m[1:2]) + m[0:1]).astype(BF16)
    pr_ref[0] = jnp.dot(h, wr_ref[...], preferred_element_type=F32).astype(pr_ref.dtype)
    pc_ref[0] = jnp.dot(h, wc_ref[...], preferred_element_type=F32).astype(pc_ref.dtype)


def _inproj_call(x, mod, in_g, in_b, w_r, w_c):
    B, T, D = x.shape
    tm = min(512, T)
    const = lambda b, t: (0, 0)
    return pl.pallas_call(
        _inproj_kernel,
        grid=(B, T // tm),
        in_specs=[pl.BlockSpec((1, tm, D), lambda b, t: (b, t, 0)),
                  pl.BlockSpec((1, 6, D), lambda b, t: (b, 0, 0)),
                  pl.BlockSpec((1, D), const), pl.BlockSpec((1, D), const),
                  pl.BlockSpec(w_r.shape, const), pl.BlockSpec(w_c.shape, const)],
        out_specs=[pl.BlockSpec((1, tm, N_SHIFT_COLS), lambda b, t: (b, t, 0)),
                   pl.BlockSpec((1, tm, 2 * D_CONV), lambda b, t: (b, t, 0))],
        out_shape=[jax.ShapeDtypeStruct((B, T, N_SHIFT_COLS), BF16),
                   jax.ShapeDtypeStruct((B, T, 2 * D_CONV), BF16)],
        compiler_params=_params(("parallel", "parallel")),
        name="inproj",
    )(x, mod, _row(in_g), _row(in_b), w_r, w_c)


def _pre_kernel(p_ref, pp_ref, pn_ref, mu_ref, w0_ref, w2_ref, a0_ref, a2_ref, gup_ref, kkw_ref, ka_ref,
                rk_ref, bd_ref, r_o, v_o, kk_o, lw_o, kd_o, bb_o, g_o, bonus_o):
    t = pl.program_id(1)
    nt = pl.num_programs(1)
    p = p_ref[0].astype(F32)
    tm = p.shape[0]
    prev_row = jnp.where(t > 0, pp_ref[0].astype(F32)[HALO - 1:HALO], 0.0)
    next_row = jnp.where(t < nt - 1, pn_ref[0].astype(F32)[0:1], 0.0)
    row = lax.broadcasted_iota(jnp.int32, p.shape, 0)
    prev = jnp.where(row == 0, prev_row, pltpu.roll(p, 1, 0))
    nxt = jnp.where(row == tm - 1, next_row, pltpu.roll(p, tm - 1, 0))
    mu = mu_ref[...]
    ps = p + mu[0:1] * (prev - p) + mu[1:2] * (nxt - p)

    r = ps[:, 0:D_RWKV]
    k = ps[:, D_RWKV:2 * D_RWKV]
    v = ps[:, 2 * D_RWKV:3 * D_RWKV]
    o = 3 * D_RWKV
    wl = ps[:, o:o + 2 * DECAY_RANK]
    al = ps[:, o + 2 * DECAY_RANK:o + 2 * DECAY_RANK + 2 * AAA_RANK]
    gl = ps[:, o + 2 * DECAY_RANK + 2 * AAA_RANK:]

    y = -(w0_ref[...] + _mm(jnp.tanh(wl), w2_ref[...]))
    softplus = jnp.maximum(y, 0.0) + jnp.log(1.0 + jnp.exp(-jnp.abs(y)))
    lw = -jnp.exp(-softplus - 0.5)
    a = _sigmoid(a0_ref[...] + _mm(al, a2_ref[...]))
    g = _mm(_sigmoid(gl), gup_ref[...])

    bd = bd_ref[...]
    kk = k * kkw_ref[...]
    ss = _mm_exact_rhs(kk * kk, bd, 2)
    kk = kk * lax.rsqrt(jnp.maximum(ss, 1e-24))
    ka = ka_ref[...]
    a0d, a1d = a[:, :D_RWKV], a[:, D_RWKV:]
    k0 = k * (1.0 + (a0d - 1.0) * ka)
    k1 = k * (1.0 + (a1d - 1.0) * ka)
    bonus = _mm_exact_rhs(r * (k0 + k1) * rk_ref[...], bd, 2) * v

    r_o[0] = r.astype(r_o.dtype)
    v_o[0] = v.astype(v_o.dtype)
    kk_o[0] = kk.astype(kk_o.dtype)
    lw_o[0] = lw
    kd_o[0, :, :D_RWKV] = k0.astype(kd_o.dtype)
    kd_o[0, :, D_RWKV:] = k1.astype(kd_o.dtype)
    bb_o[0, :, :D_RWKV] = (kk * a0d).astype(bb_o.dtype)
    bb_o[0, :, D_RWKV:] = (kk * a1d).astype(bb_o.dtype)
    g_o[0] = g.astype(g_o.dtype)
    bonus_o[0] = bonus.astype(bonus_o.dtype)


def _pre_call(p_r, mu_shift, w0c, w2c, a0c, a2c, g_up, k_k, k_a, r_k, bd):
    B, T, C = p_r.shape
    tm = min(256, T)
    nh = tm // HALO
    const = lambda b, t: (0, 0)
    tile = lambda w: pl.BlockSpec((1, tm, w), lambda b, t: (b, t, 0))
    outs = [D_RWKV, D_RWKV, D_RWKV, 2 * D_RWKV, 2 * D_RWKV, 2 * D_RWKV, D_RWKV, D_RWKV]
    return pl.pallas_call(
        _pre_kernel,
        grid=(B, T // tm),
        in_specs=[tile(C),
                  pl.BlockSpec((1, HALO, C), lambda b, t: (b, jnp.maximum(t * nh - 1, 0), 0)),
                  pl.BlockSpec((1, HALO, C), lambda b, t: (b, jnp.minimum((t + 1) * nh, T // HALO - 1), 0)),
                  pl.BlockSpec(mu_shift.shape, const), pl.BlockSpec(w0c.shape, const),
                  pl.BlockSpec(w2c.shape, const), pl.BlockSpec(a0c.shape, const),
                  pl.BlockSpec(a2c.shape, const), pl.BlockSpec(g_up.shape, const),
                  pl.BlockSpec(k_k.shape, const), pl.BlockSpec(k_a.shape, const),
                  pl.BlockSpec(r_k.shape, const), pl.BlockSpec(bd.shape, const)],
        out_specs=[tile(w) for w in outs],
        out_shape=[jax.ShapeDtypeStruct((B, T, w), F32 if i == 3 else BF16) for i, w in enumerate(outs)],
        compiler_params=_params(("parallel", "parallel")),
        name="pre",
    )(p_r, p_r, p_r, mu_shift, w0c, w2c, a0c, a2c, g_up, k_k, k_a, r_k, bd)


def _chains_chunk(chains):
    assert CHUNK == HEAD_DIM
    L = chains[0]["rt"].shape[0]
    R = GROUP * L
    rowh = lax.broadcasted_iota(jnp.int32, (R, GROUP_W), 0) // L
    colh = lax.broadcasted_iota(jnp.int32, (R, GROUP_W), 1) // HEAD_DIM
    own = rowh == colh

    def bd(x):
        return jnp.where(own, jnp.concatenate([x.astype(BF16)] * GROUP, axis=0), jnp.zeros((), BF16))

    def dot(a, b):
        return jnp.dot(a.astype(BF16), b, preferred_element_type=F32)

    t_row = lax.broadcasted_iota(jnp.int32, (L, GROUP_W), 0)
    i_col = lax.broadcasted_iota(jnp.int32, (L, GROUP_W), 1) % L
    eye = jnp.where(t_row == i_col, 1.0, 0.0)
    masks = {False: (i_col < t_row, i_col <= t_row), True: (i_col > t_row, i_col >= t_row)}

    zr = [jnp.concatenate([c["zt"], c["rt"]], axis=0).astype(BF16) for c in chains]
    a_b = [_mm_nt(x, bd(c["bt"])) for x, c in zip(zr, chains)]
    a_k = [_mm_nt(x, bd(c["kt"])) for x, c in zip(zr, chains)]
    n_zb = [jnp.where(masks[c["reverse"]][0], a[:L], 0.0) for c, a in zip(chains, a_b)]
    a_rb = [jnp.where(masks[c["reverse"]][1], a[L:], 0.0) for c, a in zip(chains, a_b)]
    a_zr = [jnp.concatenate([jnp.where(masks[c["reverse"]][0], a[:L], 0.0),
                             jnp.where(masks[c["reverse"]][1], a[L:], 0.0)], axis=0) for c, a in zip(chains, a_k)]

    tinv = [eye + n for n in n_zb]
    pw = [dot(n, bd(n)) for n in n_zb]
    levels = L.bit_length() - 2
    for j in range(1, levels + 1):
        if j < levels:
            both = [dot(jnp.concatenate([p, t], axis=0), bd(p)) for p, t in zip(pw, tinv)]
            pw = [b[:L] for b in both]
            tinv = [t + b[L:] for t, b in zip(tinv, both)]
        else:
            tinv = [t + dot(t, bd(p)) for p, t in zip(pw, tinv)]

    av = [dot(a, bd(c["v"])) for a, c in zip(a_zr, chains)]

    def rows(x):
        return jnp.concatenate([x[:, h * HEAD_DIM:(h + 1) * HEAD_DIM] for h in range(GROUP)], axis=0)

    def advance(idx, S):
        zs = [_mm_nt(zr[i], bd(s)) for i, s in zip(idx, S)]
        u = [dot(tinv[i], bd(z[:L] + av[i][:L])) for i, z in zip(idx, zs)]
        outs = [z[L:] + av[i][L:] + dot(a_rb[i], bd(x)) for i, z, x in zip(idx, zs, u)]
        nxt = [s * chains[i]["wtot"] + _mm_tn(rows(x), bd(chains[i]["bhat"]))
               + _mm_tn(rows(chains[i]["v"]), bd(chains[i]["khat"])) for i, s, x in zip(idx, S, u)]
        return outs, nxt

    return advance


def _prep_chains(r, v, kk, lw, kd, bb, reverse):
    L = r.shape[0]
    r, kk, kd, bb = (x.astype(F32) for x in (r, kk, kd, bb))
    ti = lax.broadcasted_iota(jnp.int32, (L, L), 0)
    ii = lax.broadcasted_iota(jnp.int32, (L, L), 1)
    tri = jnp.where((ii >= ti) if reverse else (ii <= ti), 1.0, 0.0).astype(BF16)
    cum = _mm_exact_lhs(tri, lw, 3)
    tot = cum[0:1] if reverse else cum[L - 1:L]
    ec, eci, ecx, eto = jnp.exp(cum), jnp.exp(-cum), jnp.exp(cum - lw), jnp.exp(tot - cum)
    full = dict(rt=r * ec, zt=-(kk * ecx), kt=kd * eci, bt=bb * eci, khat=kd * eto, bhat=bb * eto, v=v,
                wtot=jnp.exp(tot))
    chains = []
    for g in range(D_RWKV // GROUP_W):
        c = {k: x[:, g * GROUP_W:(g + 1) * GROUP_W] for k, x in full.items()}
        c["reverse"] = reverse
        chains.append(c)
    return chains


def _scan_kernel(rf, vf, kkf, lwf, kdf, bbf, rb, vb, kkb, lwb, kdb, bbb, of, ob, s_ref):
    @pl.when(pl.program_id(1) == 0)
    def _():
        s_ref[...] = jnp.zeros_like(s_ref)

    n_sub = rf.shape[1] // CHUNK
    n_grp = D_RWKV // GROUP_W
    per = 2 * n_grp
    chains, spans = [], []
    for i in range(n_sub):
        lf = pl.ds(i * CHUNK, CHUNK)
        lb = pl.ds((n_sub - 1 - i) * CHUNK, CHUNK)
        spans.append((lf, lb))
        chains += _prep_chains(rf[0, lf], vf[0, lf], kkf[0, lf], lwf[0, lf], kdf[0, lf], bbf[0, lf], False)
        chains += _prep_chains(rb[0, lb], vb[0, lb], kkb[0, lb], lwb[0, lb], kdb[0, lb], bbb[0, lb], True)
    advance = _chains_chunk(chains)
    sf, sb = s_ref[0], s_ref[1]
    states = ([sf[:, g * GROUP_W:(g + 1) * GROUP_W] for g in range(n_grp)]
              + [sb[:, g * GROUP_W:(g + 1) * GROUP_W] for g in range(n_grp)])
    for i, (lf, lb) in enumerate(spans):
        outs, states = advance(list(range(i * per, (i + 1) * per)), states)
        of[0, lf] = jnp.concatenate(outs[:n_grp], axis=1)
        ob[0, lb] = jnp.concatenate(outs[n_grp:], axis=1)
    s_ref[0] = jnp.concatenate(states[:n_grp], axis=1)
    s_ref[1] = jnp.concatenate(states[n_grp:], axis=1)


def _scan_call(r, v, kk, lw, kd, bb):
    B, T, C = r.shape
    L = min(SCAN_BLOCK, T)
    nc = T // L
    fwd = lambda j: pl.BlockSpec((1, L, C), lambda b, c: (b, c, j))
    bwd = lambda j: pl.BlockSpec((1, L, C), lambda b, c: (b, nc - 1 - c, j))
    return pl.pallas_call(
        _scan_kernel,
        grid=(B, nc),
        in_specs=[fwd(0)] * 6 + [bwd(0)] * 3 + [bwd(1)] * 3,
        out_specs=[fwd(0), bwd(0)],
        out_shape=[jax.ShapeDtypeStruct((B, T, C), F32)] * 2,
        scratch_shapes=[pltpu.VMEM((2, HEAD_DIM, C), F32)],
        compiler_params=_params(("parallel", "arbitrary")),
        name="scan",
    )(r, v, kk, lw, kd, bb, r, v, kk, lw, kd, bb)


def _conv_kernel(p_ref, pp_ref, pn_ref, w_ref, b_ref, g_ref, be_ref, o_ref, h_ref):
    t = pl.program_id(1)
    nt = pl.num_programs(1)
    tm = p_ref.shape[1]

    def glu(x):
        x = x.astype(F32)
        return x[:, :D_CONV] * _sigmoid(x[:, D_CONV:])

    h_ref[0:HALO] = jnp.where(t > 0, glu(pp_ref[0]), 0.0)
    h_ref[HALO:HALO + tm] = glu(p_ref[0])
    h_ref[HALO + tm:] = jnp.where(t < nt - 1, glu(pn_ref[0]), 0.0)
    w = w_ref[...]
    acc = jnp.zeros((tm, D_CONV), F32) + b_ref[...]
    base = HALO - CONV_WIDTH // 2
    for j in range(CONV_WIDTH):
        acc = acc + w[j:j + 1] * h_ref[base + j:base + j + tm]
    y = _ln(acc, g_ref[...], be_ref[...], LN_EPS)
    o_ref[0] = (y * _sigmoid(y)).astype(o_ref.dtype)


def _conv_call(p_c, dw_w, dw_b, cln_g, cln_b):
    B, T, C = p_c.shape
    tm = min(256, T)
    nh = tm // HALO
    const = lambda b, t: (0, 0)
    return pl.pallas_call(
        _conv_kernel,
        grid=(B, T // tm),
        in_specs=[pl.BlockSpec((1, tm, C), lambda b, t: (b, t, 0)),
                  pl.BlockSpec((1, HALO, C), lambda b, t: (b, jnp.maximum(t * nh - 1, 0), 0)),
                  pl.BlockSpec((1, HALO, C), lambda b, t: (b, jnp.minimum((t + 1) * nh, T // HALO - 1), 0)),
                  pl.BlockSpec(dw_w.shape, const), pl.BlockSpec((1, D_CONV), const),
                  pl.BlockSpec((1, D_CONV), const), pl.BlockSpec((1, D_CONV), const)],
        out_specs=pl.BlockSpec((1, tm, D_CONV), lambda b, t: (b, t, 0)),
        out_shape=jax.ShapeDtypeStruct((B, T, D_CONV), BF16),
        scratch_shapes=[pltpu.VMEM((tm + 2 * HALO, D_CONV), F32)],
        compiler_params=_params(("parallel", "parallel")),
        name="conv",
    )(p_c, p_c, p_c, dw_w, _row(dw_b), _row(cln_g), _row(cln_b))


def _post_kernel(x_ref, mod_ref, ing_ref, inb_ref, wf_ref, wb_ref, bonus_ref, g_ref, yb_ref, gng_ref, gnb_ref,
                 bd_ref, wo_ref, l1g_ref, l1b_ref, wr_ref, br_ref, x1_o, h2_o, gate_o):
    m = mod_ref[0]
    x0 = _ln(x_ref[0], ing_ref[...], inb_ref[...], LN_EPS)
    bd = bd_ref[...]
    wkv = wf_ref[0] + wb_ref[0]
    mu = _mm_exact_rhs(wkv, bd, 3) * (1.0 / HEAD_DIM)
    wc = wkv - mu
    var = _mm_exact_rhs(wc * wc, bd, 2) * (1.0 / HEAD_DIM)
    o = wc * lax.rsqrt(var + GN_EPS) * gng_ref[...] + gnb_ref[...]
    ya = (o + bonus_ref[0].astype(F32)) * g_ref[0].astype(F32)
    wo = wo_ref[...]
    mix = _mm(ya, wo[:D_RWKV]) + jnp.dot(yb_ref[0], wo[D_RWKV:], preferred_element_type=F32)
    x1 = _ln(DEEPNORM_ALPHA * x0 + (1.0 + m[2:3]) * mix, l1g_ref[...], l1b_ref[...], LN_EPS)
    x1_o[0] = x1
    h2 = x1 * (1.0 + m[4:5]) + m[3:4]
    h2_o[0] = h2.astype(h2_o.dtype)

    logits = _mm3(h2, wr_ref[...]) + br_ref[...]
    lane = lax.broadcasted_iota(jnp.int32, logits.shape, 1)
    work = logits
    sel = jnp.zeros(logits.shape, F32)
    top = None
    for _ in range(TOP_K):
        mx = jnp.max(work, -1, keepdims=True)
        first = jnp.min(jnp.where(work == mx, lane, LANES), -1, keepdims=True)
        pick = lane == first
        sel = jnp.where(pick, 1.0, sel)
        work = jnp.where(pick, -jnp.inf, work)
        top = mx if top is None else top
    e = jnp.where(sel > 0.0, jnp.exp(logits - top), 0.0)
    gates = e / jnp.sum(e, -1, keepdims=True)
    gate_o[0] = gates.T[:N_EXPERTS]


def _post_call(x, mod, in_g, in_b, wkv_f, wkv_b, bonus, g, y_b, gn_g, gn_b, bd, w_out, ln1_g, ln1_b,
               w_router_p, b_router_p):
    B, T, D = x.shape
    tm = min(256, T)
    const = lambda b, t: (0, 0)
    tile = lambda w: pl.BlockSpec((1, tm, w), lambda b, t: (b, t, 0))
    vec = lambda w: pl.BlockSpec((1, w), const)
    return pl.pallas_call(
        _post_kernel,
        grid=(B, T // tm),
        in_specs=[tile(D), pl.BlockSpec((1, 6, D), lambda b, t: (b, 0, 0)), vec(D), vec(D),
                  tile(D_RWKV), tile(D_RWKV), tile(D_RWKV), tile(D_RWKV), tile(D_CONV),
                  vec(D_RWKV), vec(D_RWKV), pl.BlockSpec(bd.shape, const),
                  pl.BlockSpec(w_out.shape, const), vec(D), vec(D),
                  pl.BlockSpec(w_router_p.shape, const), vec(LANES)],
        out_specs=[tile(D), tile(D), pl.BlockSpec((1, N_EXPERTS, tm), lambda b, t: (b, 0, t))],
        out_shape=[jax.ShapeDtypeStruct((B, T, D), F32), jax.ShapeDtypeStruct((B, T, D), BF16),
                   jax.ShapeDtypeStruct((B, N_EXPERTS, T), F32)],
        compiler_params=_params(("parallel", "parallel")),
        name="post",
    )(x, mod, _row(in_g), _row(in_b), wkv_f, wkv_b, bonus, g, y_b, _row(gn_g), _row(gn_b), bd, w_out,
      _row(ln1_g), _row(ln1_b), w_router_p, b_router_p)


def _routing(gates_t, tb):
    B, E, T = gates_t.shape
    nb = T // tb
    n_blocks = B * nb
    cnt = jnp.sum((gates_t > 0.0).reshape(B, E, nb, tb), -1, dtype=jnp.int32)
    cnt = cnt.transpose(1, 0, 2).reshape(E, n_blocks)
    cnt_al = (cnt + ROW_ALIGN - 1) // ROW_ALIGN * ROW_ALIGN
    n_e = jnp.sum(cnt_al, 1)
    seg = (n_e + CAP + FFN_TILE - 1) // FFN_TILE * FFN_TILE
    seg_end = jnp.cumsum(seg)
    seg_start = seg_end - seg
    start = seg_start[:, None] + jnp.cumsum(cnt_al, 1) - cnt_al
    z0 = seg_start + n_e // FFN_TILE * FFN_TILE
    zrow = jnp.concatenate([z0, jnp.minimum(z0 + FFN_TILE, seg_end[-1] - FFN_TILE), seg_end[-1:] // FFN_TILE])
    rows_max = TOP_K * B * T + (ROW_ALIGN - 1) * n_blocks * E + E * (CAP + FFN_TILE - 1)
    n_tiles = -(-rows_max // FFN_TILE)
    idx = jnp.arange(n_tiles, dtype=jnp.int32)
    n_valid = seg_end[-1] // FFN_TILE
    src = jnp.minimum(idx, n_valid - 1)
    tile_e = jnp.sum(seg_end[None, :] <= (src * FFN_TILE)[:, None], 1, dtype=jnp.int32)
    over = jnp.max(cnt, 0) > CAP
    return dict(start=start.reshape(-1), cnt=cnt.reshape(-1), over=over.astype(jnp.int32), zrow=zrow,
                tile_e=tile_e, src=src, valid=(idx < n_valid).astype(jnp.int32), n_tiles=n_tiles)


def _block_ranks(g, tri):
    routed = g > 0.0
    rank = jnp.dot(jnp.where(routed, 1.0, 0.0).astype(BF16), tri, preferred_element_type=F32)
    return jnp.where(routed, rank, -1.0)


def _group_copy(buf, hbm, sem, slot, row):
    return pltpu.make_async_copy(buf.at[pl.ds(slot * CAP, CAP)],
                                 hbm.at[pl.ds(pl.multiple_of(row, ROW_ALIGN), CAP)], sem.at[0])


def _dispatch_kernel(start_ref, cnt_ref, over_ref, zrow_ref, h_ref, g_ref, tri_ref, xs_out, stage, rank_s, sem):
    E, tb = g_ref.shape[1], g_ref.shape[2]
    n_blocks = pl.num_programs(0) * pl.num_programs(1)
    blk = pl.program_id(0) * pl.num_programs(1) + pl.program_id(1)

    @pl.when(blk == 0)
    def _():
        def zero_copy(i):
            return pltpu.make_async_copy(stage.at[pl.ds(0, FFN_TILE)],
                                         xs_out.at[pl.ds(pl.multiple_of(zrow_ref[i], FFN_TILE), FFN_TILE)], sem.at[0])
        def zero_tile(i, carry):
            cp = pltpu.make_async_copy(stage.at[pl.ds(0, FFN_TILE)],
                                       xs_out.at[pl.ds(pl.multiple_of(i * FFN_TILE, FFN_TILE), FFN_TILE)], sem.at[0])
            cp.start()
            cp.wait()
            return carry

        stage[0:FFN_TILE] = jnp.zeros((FFN_TILE, stage.shape[1]), stage.dtype)
        for i in range(E):
            zero_copy(i).start()
        for i in range(E):
            zero_copy(i).wait()
        for i in range(E, 2 * E):
            cp = zero_copy(i)
            cp.start()
            cp.wait()
        lax.fori_loop(zrow_ref[2 * E], xs_out.shape[0] // FFN_TILE, zero_tile, 0)

    rank = _block_ranks(g_ref[0], tri_ref[...])
    rank_s[...] = rank
    slot = lax.broadcasted_iota(jnp.int32, (CAP, tb), 0).astype(F32)
    h = h_ref[0]
    n_part = 4
    per = E // n_part
    for q in range(n_part):
        sel = jnp.concatenate([jnp.where(rank[e:e + 1] == slot, 1.0, 0.0).astype(BF16)
                               for e in range(q * per, (q + 1) * per)], axis=0)
        stage[q * per * CAP:(q + 1) * per * CAP] = jnp.dot(sel, h, preferred_element_type=F32).astype(BF16)
        for e in range(q * per, (q + 1) * per):
            _group_copy(stage, xs_out, sem, e, start_ref[e * n_blocks + blk]).start()
    for e in range(E):
        _group_copy(stage, xs_out, sem, e, start_ref[e * n_blocks + blk]).wait()

    @pl.when(over_ref[blk] > 0)
    def _():
        def per_expert(e, carry):
            def per_chunk(c, carry):
                row = rank_s[pl.ds(e, 1), :] - (c * CAP).astype(F32)
                sel = jnp.where(row == slot, 1.0, 0.0).astype(BF16)
                stage[0:CAP] = jnp.dot(sel, h, preferred_element_type=F32).astype(BF16)
                cp = _group_copy(stage, xs_out, sem, 0, start_ref[e * n_blocks + blk] + c * CAP)
                cp.start()
                cp.wait()
                return carry
            n_chunks = (cnt_ref[e * n_blocks + blk] + CAP - 1) // CAP
            return lax.fori_loop(1, n_chunks, per_chunk, carry)
        lax.fori_loop(0, E, per_expert, 0)


def _dispatch_call(h2, gates_t, tri, rt):
    B, T, D = h2.shape
    E = gates_t.shape[1]
    tb = tri.shape[0]
    rows = rt["n_tiles"] * FFN_TILE
    assert E * CAP >= FFN_TILE
    grid_spec = pltpu.PrefetchScalarGridSpec(
        num_scalar_prefetch=4,
        grid=(B, T // tb),
        in_specs=[pl.BlockSpec((1, tb, D), lambda b, t, *_: (b, t, 0)),
                  pl.BlockSpec((1, E, tb), lambda b, t, *_: (b, 0, t)),
                  pl.BlockSpec((tb, tb), lambda b, t, *_: (0, 0))],
        out_specs=pl.BlockSpec(memory_space=pl.ANY),
        scratch_shapes=[pltpu.VMEM((E * CAP, D), BF16), pltpu.VMEM((E, tb), F32), pltpu.SemaphoreType.DMA((1,))],
    )
    return pl.pallas_call(
        _dispatch_kernel, grid_spec=grid_spec,
        out_shape=jax.ShapeDtypeStruct((rows, D), BF16),
        compiler_params=_params(("arbitrary", "arbitrary")),
        name="dispatch",
    )(rt["start"], rt["cnt"], rt["over"], rt["zrow"], h2, gates_t, tri)


def _ffn_kernel(te_ref, src_ref, valid_ref, x_ref, wgu_ref, bgu_ref, wd_ref, bdn_ref, o_ref):
    i = pl.program_id(0)

    @pl.when(valid_ref[i] > 0)
    def _():
        h = jnp.dot(x_ref[...], wgu_ref[0], preferred_element_type=F32) + bgu_ref[0]
        h_glu = jnp.minimum(h[:, :D_FF], SWIGLU_LIMIT)
        h_lin = jnp.clip(h[:, D_FF:], -SWIGLU_LIMIT, SWIGLU_LIMIT)
        y = (h_lin + 1.0) * (h_glu * _sigmoid(SWIGLU_ALPHA * h_glu))
        o_ref[...] = (_mm(y, wd_ref[0]) + bdn_ref[0]).astype(o_ref.dtype)

    @pl.when(valid_ref[i] == 0)
    def _():
        o_ref[...] = jnp.zeros_like(o_ref)


def _ffn_call(xs, rt, w_gu, b_gu, w_down, b_down):
    rows, D = xs.shape
    E = w_gu.shape[0]
    grid_spec = pltpu.PrefetchScalarGridSpec(
        num_scalar_prefetch=3,
        grid=(rt["n_tiles"],),
        in_specs=[pl.BlockSpec((FFN_TILE, D), lambda i, te, src, valid: (src[i], 0)),
                  pl.BlockSpec((1, D, 2 * D_FF), lambda i, te, src, valid: (te[i], 0, 0)),
                  pl.BlockSpec((1, 1, 2 * D_FF), lambda i, te, src, valid: (te[i], 0, 0)),
                  pl.BlockSpec((1, D_FF, D), lambda i, te, src, valid: (te[i], 0, 0)),
                  pl.BlockSpec((1, 1, D), lambda i, te, src, valid: (te[i], 0, 0))],
        out_specs=pl.BlockSpec((FFN_TILE, D), lambda i, te, src, valid: (i, 0)),
    )
    return pl.pallas_call(
        _ffn_kernel, grid_spec=grid_spec,
        out_shape=jax.ShapeDtypeStruct((rows, D), BF16),
        compiler_params=_params(("arbitrary",)),
        name="ffn",
    )(rt["tile_e"], rt["src"], rt["valid"], xs, w_gu, b_gu.reshape(E, 1, -1), w_down,
      b_down.reshape(E, 1, -1))


def _combine_kernel(start_ref, cnt_ref, over_ref, g_ref, x1_ref, mod_ref, tri_ref, lg_ref, lb_ref, ys_ref, o_ref,
                    ybuf, xbuf, rank_s, acc_s, sem):
    E, tb = g_ref.shape[1], g_ref.shape[2]
    n_blocks = pl.num_programs(0) * pl.num_programs(1)
    blk = pl.program_id(0) * pl.num_programs(1) + pl.program_id(1)
    cur = blk % 2

    def fetch(half, e, block):
        row = start_ref[e * n_blocks + block]
        return pltpu.make_async_copy(ys_ref.at[pl.ds(pl.multiple_of(row, ROW_ALIGN), CAP)],
                                     ybuf.at[half, pl.ds(e * CAP, CAP)], sem.at[half])

    @pl.when(blk == 0)
    def _():
        for e in range(E):
            fetch(0, e, 0).start()

    @pl.when(blk + 1 < n_blocks)
    def _():
        for e in range(E):
            fetch(1 - cur, e, blk + 1).start()

    g = g_ref[0]
    rank = _block_ranks(g, tri_ref[...])
    slot = lax.broadcasted_iota(jnp.int32, (CAP, tb), 0).astype(F32)
    for e in range(E):
        fetch(cur, e, blk).wait()
    n_part = 4
    per = E // n_part
    acc = None
    for q in range(n_part):
        w = jnp.concatenate([jnp.where(rank[e:e + 1] == slot, g[e:e + 1], 0.0).astype(BF16)
                             for e in range(q * per, (q + 1) * per)], axis=0)
        part = _mm_tn(w, ybuf[cur, q * per * CAP:(q + 1) * per * CAP])
        acc = part if acc is None else acc + part
    acc_s[...] = acc

    @pl.when(over_ref[blk] > 0)
    def _():
        rank_s[...] = rank

        def per_expert(e, carry):
            def per_chunk(c, carry):
                row0 = start_ref[e * n_blocks + blk] + c * CAP
                cp = pltpu.make_async_copy(ys_ref.at[pl.ds(pl.multiple_of(row0, ROW_ALIGN), CAP)], xbuf, sem.at[2])
                cp.start()
                row = rank_s[pl.ds(e, 1), :] - (c * CAP).astype(F32)
                w = jnp.where(row == slot, g_ref[0, pl.ds(e, 1), :], 0.0).astype(BF16)
                cp.wait()
                acc_s[...] += _mm_tn(w, xbuf[...])
                return carry
            n_chunks = (cnt_ref[e * n_blocks + blk] + CAP - 1) // CAP
            return lax.fori_loop(1, n_chunks, per_chunk, carry)
        lax.fori_loop(0, E, per_expert, 0)

    m = mod_ref[0]
    o_ref[0] = _ln(DEEPNORM_ALPHA * x1_ref[0] + (1.0 + m[5:6]) * acc_s[...], lg_ref[...], lb_ref[...], LN_EPS)


def _combine_call(ys, gates_t, x1, mod, tri, rt, ln2_g, ln2_b):
    B, T, D = x1.shape
    E = gates_t.shape[1]
    tb = tri.shape[0]
    const = lambda b, t, *_: (0, 0)
    grid_spec = pltpu.PrefetchScalarGridSpec(
        num_scalar_prefetch=3,
        grid=(B, T // tb),
        in_specs=[pl.BlockSpec((1, E, tb), lambda b, t, *_: (b, 0, t)),
                  pl.BlockSpec((1, tb, D), lambda b, t, *_: (b, t, 0)),
                  pl.BlockSpec((1, 6, D), lambda b, t, *_: (b, 0, 0)),
                  pl.BlockSpec((tb, tb), const),
                  pl.BlockSpec((1, D), const), pl.BlockSpec((1, D), const),
                  pl.BlockSpec(memory_space=pl.ANY)],
        out_specs=pl.BlockSpec((1, tb, D), lambda b, t, *_: (b, t, 0)),
        scratch_shapes=[pltpu.VMEM((2, E * CAP, D), BF16), pltpu.VMEM((CAP, D), BF16), pltpu.VMEM((E, tb), F32),
                        pltpu.VMEM((tb, D), F32), pltpu.SemaphoreType.DMA((3,))],
    )
    return pl.pallas_call(
        _combine_kernel, grid_spec=grid_spec,
        out_shape=jax.ShapeDtypeStruct((B, T, D), F32),
        compiler_params=_params(("arbitrary", "arbitrary")),
        name="combine",
    )(rt["start"], rt["cnt"], rt["over"], gates_t, x1, mod, tri, _row(ln2_g), _row(ln2_b), ys)


def _moe_call(h2, gates_t, x1, mod, w_gu, b_gu, w_down, b_down, ln2_g, ln2_b):
    tb = min(TOKEN_BLOCK, h2.shape[1])
    pos = lax.broadcasted_iota(jnp.int32, (tb, tb), 0)
    tri = (pos < pos.T).astype(BF16)
    rt = _routing(gates_t, tb)
    xs = _dispatch_call(h2, gates_t, tri, rt)
    ys = _ffn_call(xs, rt, w_gu, b_gu, w_down, b_down)
    return _combine_call(ys, gates_t, x1, mod, tri, rt, ln2_g, ln2_b)


def _block_diag_cat(w):
    z = jnp.zeros_like(w[0])
    return jnp.concatenate([jnp.concatenate([w[0], z], 1), jnp.concatenate([z, w[1]], 1)], 0)


def _encode(x, mod, W):
    p_r, p_c = _inproj_call(x, mod, W["in_g"], W["in_b"], W["w_in_r"], W["w_in_c"])
    r, v, kk, lw, kd, bb, g, bonus = _pre_call(p_r, W["mu_shift"], W["w0c"], W["w2c"], W["a0c"], W["a2c"],
                                               W["g_up"], W["k_k"], W["k_a"], W["r_k"], W["bd"])
    wkv_f, wkv_b = _scan_call(r, v, kk, lw, kd, bb)
    y_b = _conv_call(p_c, W["dw_w"], W["dw_b"], W["cln_g"], W["cln_b"])
    x1, h2, gates = _post_call(x, mod, W["in_g"], W["in_b"], wkv_f, wkv_b, bonus, g, y_b, W["gn_g"], W["gn_b"],
                               W["bd"], W["w_out"], W["ln1_g"], W["ln1_b"], W["w_router_p"], W["b_router_p"])
    return _moe_call(h2, gates, x1, mod, W["w_gu"], W["b_gu"], W["w_down"], W["b_down"], W["ln2_g"], W["ln2_b"])


def kernel(x_prompt, x_sample, c_prompt, c_sample, in_g, in_b, w_mod, b_mod, w_in, mu_shift, w0, w2, a0, a2, g_up, k_k, k_a, r_k, gn_g, gn_b, dw_w, dw_b, cln_g, cln_b, w_out, ln1_g, ln1_b, w_router, b_router, w_gu, b_gu, w_down, b_down, ln2_g, ln2_b):
    l = 0
    head = lax.broadcasted_iota(jnp.int32, (D_RWKV, D_RWKV), 0) // HEAD_DIM
    W = dict(
        in_g=in_g, in_b=in_b,
        w_in_r=w_in[l][:, :N_SHIFT_COLS].astype(BF16), w_in_c=w_in[l][:, N_SHIFT_COLS:].astype(BF16),
        mu_shift=mu_shift[l],
        w0c=w0[l].reshape(1, -1), w2c=_block_diag_cat(w2[l]).astype(BF16),
        a0c=a0[l].reshape(1, -1), a2c=_block_diag_cat(a2[l]).astype(BF16),
        g_up=g_up[l].astype(BF16), k_k=_row(k_k[l]), k_a=_row(k_a[l]), r_k=_row(r_k[l]),
        bd=(head == head.T).astype(BF16),
        gn_g=gn_g[l], gn_b=gn_b[l], dw_w=dw_w[l], dw_b=dw_b[l], cln_g=cln_g[l], cln_b=cln_b[l],
        w_out=w_out[l].astype(BF16), ln1_g=ln1_g[l], ln1_b=ln1_b[l],
        w_router_p=jnp.pad(w_router[l], ((0, 0), (0, LANES - N_EXPERTS))),
        b_router_p=jnp.pad(b_router[l], (0, LANES - N_EXPERTS), constant_values=-1e30).reshape(1, -1),
        w_gu=w_gu[l].astype(BF16), b_gu=b_gu[l], w_down=w_down[l].astype(BF16), b_down=b_down[l],
        ln2_g=ln2_g[l], ln2_b=ln2_b[l],
    )
    nb = x_prompt.shape[0]
    mod = _mod_call(jnp.concatenate([c_prompt, c_sample], 0), w_mod[l], b_mod[l])
    return (_encode(x_prompt, mod[:nb], W), _encode(x_sample, mod[nb:], W))
```

```python
import functools

import jax
import jax.numpy as jnp
from jax import lax
from jax.experimental import pallas as pl
from jax.experimental.pallas import tpu as pltpu

F32 = jnp.float32
BF16 = jnp.bfloat16

D_MODEL = 1024
D_RWKV = 512
HEAD_DIM = 64
N_HEADS = 8
D_CONV = 512
DECAY_RANK = 64
AAA_RANK = 64
GATE_RANK = 128
CONV_WIDTH = 31
N_EXPERTS = 32
TOP_K = 4
D_FF = 1024
SWIGLU_LIMIT = 7.0
SWIGLU_ALPHA = 1.702
LN_EPS = 1e-5
GN_EPS = 64e-5
DEPTH = 1
DEEPNORM_ALPHA = (2.0 * DEPTH) ** 0.25
N_SHIFT_COLS = 3 * D_RWKV + 2 * DECAY_RANK + 2 * AAA_RANK + GATE_RANK

LANES = 128
HALO = 16
CHUNK = 64
SCAN_BLOCK = 8 * CHUNK
TOKEN_BLOCK = 512
CAP = 128
ROW_ALIGN = 16
FFN_TILE = 1024
GROUP = 4
GROUP_W = GROUP * HEAD_DIM
VMEM_LIMIT = 56 * 1024 * 1024


def _params(sem, flags=None):
    return pltpu.CompilerParams(dimension_semantics=sem, vmem_limit_bytes=VMEM_LIMIT, flags=flags)


def _sigmoid(x):
    return 1.0 / (1.0 + jnp.exp(-x))


def _mm(a, b):
    return jnp.dot(a.astype(BF16), b.astype(BF16), preferred_element_type=F32)


def _mm_nt(a, b):
    return lax.dot_general(a.astype(BF16), b.astype(BF16), (((1,), (1,)), ((), ())),
                           preferred_element_type=F32)


def _mm_tn(a, b):
    return lax.dot_general(a.astype(BF16), b.astype(BF16), (((0,), (0,)), ((), ())),
                           preferred_element_type=F32)


def _split(a, n):
    parts = []
    for _ in range(n):
        h = a.astype(BF16)
        parts.append(h)
        a = a - h.astype(F32)
    return parts


def _mm_exact_rhs(a, b_bf16, n):
    acc = None
    for h in _split(a, n):
        t = jnp.dot(h, b_bf16, preferred_element_type=F32)
        acc = t if acc is None else acc + t
    return acc


def _mm_exact_lhs(a_bf16, b, n):
    acc = None
    for h in _split(b, n):
        t = jnp.dot(a_bf16, h, preferred_element_type=F32)
        acc = t if acc is None else acc + t
    return acc


def _mm3(a, b):
    ah, al = _split(a, 2)
    bh, bl = _split(b, 2)
    return (jnp.dot(ah, bh, preferred_element_type=F32) + jnp.dot(al, bh, preferred_element_type=F32)
            + jnp.dot(ah, bl, preferred_element_type=F32))


def _ln(x, g, b, eps):
    mu = jnp.mean(x, -1, keepdims=True)
    xc = x - mu
    var = jnp.mean(xc * xc, -1, keepdims=True)
    return xc * lax.rsqrt(var + eps) * g + b


def _row(a):
    return a.reshape(1, -1)


def _mod_kernel(c_ref, w_ref, b_ref, o_ref):
    c = c_ref[...]
    o_ref[...] = _mm3(c * _sigmoid(c), w_ref[...]) + b_ref[...]


def _mod_call(c, w_mod, b_mod):
    n, d = c.shape
    npad = -(-n // 8) * 8
    cp = jnp.pad(c, ((0, npad - n), (0, 0)))
    tn = 1536
    out = pl.pallas_call(
        _mod_kernel,
        grid=(6 * d // tn,),
        in_specs=[pl.BlockSpec((npad, d), lambda j: (0, 0)),
                  pl.BlockSpec((d, tn), lambda j: (0, j)),
                  pl.BlockSpec((1, tn), lambda j: (0, j))],
        out_specs=pl.BlockSpec((npad, tn), lambda j: (0, j)),
        out_shape=jax.ShapeDtypeStruct((npad, 6 * d), F32),
        compiler_params=_params(("arbitrary",)),
        name="mod",
    )(cp, w_mod, _row(b_mod))
    return out[:n].reshape(n, 6, d)


def _inproj_kernel(x_ref, mod_ref, g_ref, b_ref, wr_ref, wc_ref, pr_ref, pc_ref):
    x0 = _ln(x_ref[0], g_ref[...], b_ref[...], LN_EPS)
    m = mod_ref[0]
    h = (x0 * (1.0 + m[1:2]) + m[0:1]).astype(BF16)
    pr_ref[0] = jnp.dot(h, wr_ref[...], preferred_element_type=F32).astype(pr_ref.dtype)
    pc_ref[0] = jnp.dot(h, wc_ref[...], preferred_element_type=F32).astype(pc_ref.dtype)


def _inproj_call(x, mod, in_g, in_b, w_r, w_c):
    B, T, D = x.shape
    tm = min(512, T)
    const = lambda b, t: (0, 0)
    return pl.pallas_call(
        _inproj_kernel,
        grid=(B, T // tm),
        in_specs=[pl.BlockSpec((1, tm, D), lambda b, t: (b, t, 0)),
                  pl.BlockSpec((1, 6, D), lambda b, t: (b, 0, 0)),
                  pl.BlockSpec((1, D), const), pl.BlockSpec((1, D), const),
                  pl.BlockSpec(w_r.shape, const), pl.BlockSpec(w_c.shape, const)],
        out_specs=[pl.BlockSpec((1, tm, N_SHIFT_COLS), lambda b, t: (b, t, 0)),
                   pl.BlockSpec((1, tm, 2 * D_CONV), lambda b, t: (b, t, 0))],
        out_shape=[jax.ShapeDtypeStruct((B, T, N_SHIFT_COLS), BF16),
                   jax.ShapeDtypeStruct((B, T, 2 * D_CONV), BF16)],
        compiler_params=_params(("parallel", "parallel")),
        name="inproj",
    )(x, mod, _row(in_g), _row(in_b), w_r, w_c)


def _pre_kernel(p_ref, pp_ref, pn_ref, mu_ref, w0_ref, w2_ref, a0_ref, a2_ref, gup_ref, kkw_ref, ka_ref,
                rk_ref, bd_ref, r_o, v_o, kk_o, lw_o, kd_o, bb_o, g_o, bonus_o):
    t = pl.program_id(1)
    nt = pl.num_programs(1)
    p = p_ref[0].astype(F32)
    tm = p.shape[0]
    prev_row = jnp.where(t > 0, pp_ref[0].astype(F32)[HALO - 1:HALO], 0.0)
    next_row = jnp.where(t < nt - 1, pn_ref[0].astype(F32)[0:1], 0.0)
    row = lax.broadcasted_iota(jnp.int32, p.shape, 0)
    prev = jnp.where(row == 0, prev_row, pltpu.roll(p, 1, 0))
    nxt = jnp.where(row == tm - 1, next_row, pltpu.roll(p, tm - 1, 0))
    mu = mu_ref[...]
    ps = p + mu[0:1] * (prev - p) + mu[1:2] * (nxt - p)

    r = ps[:, 0:D_RWKV]
    k = ps[:, D_RWKV:2 * D_RWKV]
    v = ps[:, 2 * D_RWKV:3 * D_RWKV]
    o = 3 * D_RWKV
    wl = ps[:, o:o + 2 * DECAY_RANK]
    al = ps[:, o + 2 * DECAY_RANK:o + 2 * DECAY_RANK + 2 * AAA_RANK]
    gl = ps[:, o + 2 * DECAY_RANK + 2 * AAA_RANK:]

    y = -(w0_ref[...] + _mm(jnp.tanh(wl), w2_ref[...]))
    softplus = jnp.maximum(y, 0.0) + jnp.log(1.0 + jnp.exp(-jnp.abs(y)))
    lw = -jnp.exp(-softplus - 0.5)
    a = _sigmoid(a0_ref[...] + _mm(al, a2_ref[...]))
    g = _mm(_sigmoid(gl), gup_ref[...])

    bd = bd_ref[...]
    kk = k * kkw_ref[...]
    ss = _mm_exact_rhs(kk * kk, bd, 1)
    kk = kk * lax.rsqrt(jnp.maximum(ss, 1e-24))
    ka = ka_ref[...]
    a0d, a1d = a[:, :D_RWKV], a[:, D_RWKV:]
    k0 = k * (1.0 + (a0d - 1.0) * ka)
    k1 = k * (1.0 + (a1d - 1.0) * ka)
    bonus = _mm_exact_rhs(r * (k0 + k1) * rk_ref[...], bd, 1) * v

    r_o[0] = r.astype(r_o.dtype)
    v_o[0] = v.astype(v_o.dtype)
    kk_o[0] = kk.astype(kk_o.dtype)
    lw_o[0] = lw
    kd_o[0, :, :D_RWKV] = k0.astype(kd_o.dtype)
    kd_o[0, :, D_RWKV:] = k1.astype(kd_o.dtype)
    bb_o[0, :, :D_RWKV] = (kk * a0d).astype(bb_o.dtype)
    bb_o[0, :, D_RWKV:] = (kk * a1d).astype(bb_o.dtype)
    g_o[0] = g.astype(g_o.dtype)
    bonus_o[0] = bonus.astype(bonus_o.dtype)


def _pre_call(p_r, mu_shift, w0c, w2c, a0c, a2c, g_up, k_k, k_a, r_k, bd):
    B, T, C = p_r.shape
    tm = min(256, T)
    nh = tm // HALO
    const = lambda b, t: (0, 0)
    tile = lambda w: pl.BlockSpec((1, tm, w), lambda b, t: (b, t, 0))
    outs = [D_RWKV, D_RWKV, D_RWKV, 2 * D_RWKV, 2 * D_RWKV, 2 * D_RWKV, D_RWKV, D_RWKV]
    return pl.pallas_call(
        _pre_kernel,
        grid=(B, T // tm),
        in_specs=[tile(C),
                  pl.BlockSpec((1, HALO, C), lambda b, t: (b, jnp.maximum(t * nh - 1, 0), 0)),
                  pl.BlockSpec((1, HALO, C), lambda b, t: (b, jnp.minimum((t + 1) * nh, T // HALO - 1), 0)),
                  pl.BlockSpec(mu_shift.shape, const), pl.BlockSpec(w0c.shape, const),
                  pl.BlockSpec(w2c.shape, const), pl.BlockSpec(a0c.shape, const),
                  pl.BlockSpec(a2c.shape, const), pl.BlockSpec(g_up.shape, const),
                  pl.BlockSpec(k_k.shape, const), pl.BlockSpec(k_a.shape, const),
                  pl.BlockSpec(r_k.shape, const), pl.BlockSpec(bd.shape, const)],
        out_specs=[tile(w) for w in outs],
        out_shape=[jax.ShapeDtypeStruct((B, T, w), F32 if i == 3 else BF16) for i, w in enumerate(outs)],
        compiler_params=_params(("parallel", "parallel")),
        name="pre",
    )(p_r, p_r, p_r, mu_shift, w0c, w2c, a0c, a2c, g_up, k_k, k_a, r_k, bd)


def _chains_chunk(chains):
    assert CHUNK == HEAD_DIM
    L = chains[0]["rt"].shape[0]
    R = GROUP * L
    rowh = lax.broadcasted_iota(jnp.int32, (R, GROUP_W), 0) // L
    colh = lax.broadcasted_iota(jnp.int32, (R, GROUP_W), 1) // HEAD_DIM
    own = rowh == colh

    def bd(x):
        return jnp.where(own, jnp.concatenate([x.astype(BF16)] * GROUP, axis=0), jnp.zeros((), BF16))

    def dot(a, b):
        return jnp.dot(a.astype(BF16), b, preferred_element_type=F32)

    t_row = lax.broadcasted_iota(jnp.int32, (L, GROUP_W), 0)
    i_col = lax.broadcasted_iota(jnp.int32, (L, GROUP_W), 1) % L
    eye = jnp.where(t_row == i_col, 1.0, 0.0)
    masks = {False: (i_col < t_row, i_col <= t_row), True: (i_col > t_row, i_col >= t_row)}

    zr = [jnp.concatenate([c["zt"], c["rt"]], axis=0).astype(BF16) for c in chains]
    a_b = [_mm_nt(x, bd(c["bt"])) for x, c in zip(zr, chains)]
    a_k = [_mm_nt(x, bd(c["kt"])) for x, c in zip(zr, chains)]
    n_zb = [jnp.where(masks[c["reverse"]][0], a[:L], 0.0) for c, a in zip(chains, a_b)]
    a_rb = [jnp.where(masks[c["reverse"]][1], a[L:], 0.0) for c, a in zip(chains, a_b)]
    a_zr = [jnp.concatenate([jnp.where(masks[c["reverse"]][0], a[:L], 0.0),
                             jnp.where(masks[c["reverse"]][1], a[L:], 0.0)], axis=0) for c, a in zip(chains, a_k)]

    tinv = [eye + n for n in n_zb]
    pw = [dot(n, bd(n)) for n in n_zb]
    levels = L.bit_length() - 2
    for j in range(1, levels + 1):
        if j < levels:
            both = [dot(jnp.concatenate([p, t], axis=0), bd(p)) for p, t in zip(pw, tinv)]
            pw = [b[:L] for b in both]
            tinv = [t + b[L:] for t, b in zip(tinv, both)]
        else:
            tinv = [t + dot(t, bd(p)) for p, t in zip(pw, tinv)]

    av = [dot(a, bd(c["v"])) for a, c in zip(a_zr, chains)]

    def rows(x):
        return jnp.concatenate([x[:, h * HEAD_DIM:(h + 1) * HEAD_DIM] for h in range(GROUP)], axis=0)

    def advance(idx, S):
        zs = [_mm_nt(zr[i], bd(s)) for i, s in zip(idx, S)]
        u = [dot(tinv[i], bd(z[:L] + av[i][:L])) for i, z in zip(idx, zs)]
        outs = [z[L:] + av[i][L:] + dot(a_rb[i], bd(x)) for i, z, x in zip(idx, zs, u)]
        nxt = [s * chains[i]["wtot"] + _mm_tn(rows(x), bd(chains[i]["bhat"]))
               + _mm_tn(rows(chains[i]["v"]), bd(chains[i]["khat"])) for i, s, x in zip(idx, S, u)]
        return outs, nxt

    return advance


def _prep_chains(r, v, kk, lw, kd, bb, reverse):
    L = r.shape[0]
    r, kk, kd, bb = (x.astype(F32) for x in (r, kk, kd, bb))
    ti = lax.broadcasted_iota(jnp.int32, (L, L), 0)
    ii = lax.broadcasted_iota(jnp.int32, (L, L), 1)
    tri = jnp.where((ii >= ti) if reverse else (ii <= ti), 1.0, 0.0).astype(BF16)
    cum = _mm_exact_lhs(tri, lw, 3)
    tot = cum[0:1] if reverse else cum[L - 1:L]
    ec, eci, ecx, eto = jnp.exp(cum), jnp.exp(-cum), jnp.exp(cum - lw), jnp.exp(tot - cum)
    full = dict(rt=r * ec, zt=-(kk * ecx), kt=kd * eci, bt=bb * eci, khat=kd * eto, bhat=bb * eto, v=v,
                wtot=jnp.exp(tot))
    chains = []
    for g in range(D_RWKV // GROUP_W):
        c = {k: x[:, g * GROUP_W:(g + 1) * GROUP_W] for k, x in full.items()}
        c["reverse"] = reverse
        chains.append(c)
    return chains


def _scan_kernel(rf, vf, kkf, lwf, kdf, bbf, rb, vb, kkb, lwb, kdb, bbb, of, ob, s_ref):
    @pl.when(pl.program_id(1) == 0)
    def _():
        s_ref[...] = jnp.zeros_like(s_ref)

    n_sub = rf.shape[1] // CHUNK
    n_grp = D_RWKV // GROUP_W
    per = 2 * n_grp
    chains, spans = [], []
    for i in range(n_sub):
        lf = pl.ds(i * CHUNK, CHUNK)
        lb = pl.ds((n_sub - 1 - i) * CHUNK, CHUNK)
        spans.append((lf, lb))
        chains += _prep_chains(rf[0, lf], vf[0, lf], kkf[0, lf], lwf[0, lf], kdf[0, lf], bbf[0, lf], False)
        chains += _prep_chains(rb[0, lb], vb[0, lb], kkb[0, lb], lwb[0, lb], kdb[0, lb], bbb[0, lb], True)
    advance = _chains_chunk(chains)
    sf, sb = s_ref[0], s_ref[1]
    states = ([sf[:, g * GROUP_W:(g + 1) * GROUP_W] for g in range(n_grp)]
              + [sb[:, g * GROUP_W:(g + 1) * GROUP_W] for g in range(n_grp)])
    for i, (lf, lb) in enumerate(spans):
        outs, states = advance(list(range(i * per, (i + 1) * per)), states)
        of[0, lf] = jnp.concatenate(outs[:n_grp], axis=1)
        ob[0, lb] = jnp.concatenate(outs[n_grp:], axis=1)
    s_ref[0] = jnp.concatenate(states[:n_grp], axis=1)
    s_ref[1] = jnp.concatenate(states[n_grp:], axis=1)


def _scan_call(r, v, kk, lw, kd, bb):
    B, T, C = r.shape
    L = min(SCAN_BLOCK, T)
    nc = T // L
    fwd = lambda j: pl.BlockSpec((1, L, C), lambda b, c: (b, c, j))
    bwd = lambda j: pl.BlockSpec((1, L, C), lambda b, c: (b, nc - 1 - c, j))
    return pl.pallas_call(
        _scan_kernel,
        grid=(B, nc),
        in_specs=[fwd(0)] * 6 + [bwd(0)] * 3 + [bwd(1)] * 3,
        out_specs=[fwd(0), bwd(0)],
        out_shape=[jax.ShapeDtypeStruct((B, T, C), F32)] * 2,
        scratch_shapes=[pltpu.VMEM((2, HEAD_DIM, C), F32)],
        compiler_params=_params(("parallel", "arbitrary")),
        name="scan",
    )(r, v, kk, lw, kd, bb, r, v, kk, lw, kd, bb)


def _conv_kernel(p_ref, pp_ref, pn_ref, w_ref, b_ref, g_ref, be_ref, o_ref, h_ref):
    t = pl.program_id(1)
    nt = pl.num_programs(1)
    tm = p_ref.shape[1]

    def glu(x):
        x = x.astype(F32)
        return x[:, :D_CONV] * _sigmoid(x[:, D_CONV:])

    h_ref[0:HALO] = jnp.where(t > 0, glu(pp_ref[0]), 0.0)
    h_ref[HALO:HALO + tm] = glu(p_ref[0])
    h_ref[HALO + tm:] = jnp.where(t < nt - 1, glu(pn_ref[0]), 0.0)
    w = w_ref[...]
    acc = jnp.zeros((tm, D_CONV), F32) + b_ref[...]
    base = HALO - CONV_WIDTH // 2
    for j in range(CONV_WIDTH):
        acc = acc + w[j:j + 1] * h_ref[base + j:base + j + tm]
    y = _ln(acc, g_ref[...], be_ref[...], LN_EPS)
    o_ref[0] = (y * _sigmoid(y)).astype(o_ref.dtype)


def _conv_call(p_c, dw_w, dw_b, cln_g, cln_b):
    B, T, C = p_c.shape
    tm = min(256, T)
    nh = tm // HALO
    const = lambda b, t: (0, 0)
    return pl.pallas_call(
        _conv_kernel,
        grid=(B, T // tm),
        in_specs=[pl.BlockSpec((1, tm, C), lambda b, t: (b, t, 0)),
                  pl.BlockSpec((1, HALO, C), lambda b, t: (b, jnp.maximum(t * nh - 1, 0), 0)),
                  pl.BlockSpec((1, HALO, C), lambda b, t: (b, jnp.minimum((t + 1) * nh, T // HALO - 1), 0)),
                  pl.BlockSpec(dw_w.shape, const), pl.BlockSpec((1, D_CONV), const),
                  pl.BlockSpec((1, D_CONV), const), pl.BlockSpec((1, D_CONV), const)],
        out_specs=pl.BlockSpec((1, tm, D_CONV), lambda b, t: (b, t, 0)),
        out_shape=jax.ShapeDtypeStruct((B, T, D_CONV), BF16),
        scratch_shapes=[pltpu.VMEM((tm + 2 * HALO, D_CONV), F32)],
        compiler_params=_params(("parallel", "parallel")),
        name="conv",
    )(p_c, p_c, p_c, dw_w, _row(dw_b), _row(cln_g), _row(cln_b))


def _post_kernel(x_ref, mod_ref, ing_ref, inb_ref, wf_ref, wb_ref, bonus_ref, g_ref, yb_ref, gng_ref, gnb_ref,
                 bd_ref, wo_ref, l1g_ref, l1b_ref, wr_ref, br_ref, x1_o, h2_o, gate_o):
    m = mod_ref[0]
    x0 = _ln(x_ref[0], ing_ref[...], inb_ref[...], LN_EPS)
    bd = bd_ref[...]
    wkv = wf_ref[0] + wb_ref[0]
    mu = _mm_exact_rhs(wkv, bd, 2) * (1.0 / HEAD_DIM)
    wc = wkv - mu
    var = _mm_exact_rhs(wc * wc, bd, 1) * (1.0 / HEAD_DIM)
    o = wc * lax.rsqrt(var + GN_EPS) * gng_ref[...] + gnb_ref[...]
    ya = (o + bonus_ref[0].astype(F32)) * g_ref[0].astype(F32)
    wo = wo_ref[...]
    mix = _mm(ya, wo[:D_RWKV]) + jnp.dot(yb_ref[0], wo[D_RWKV:], preferred_element_type=F32)
    x1 = _ln(DEEPNORM_ALPHA * x0 + (1.0 + m[2:3]) * mix, l1g_ref[...], l1b_ref[...], LN_EPS)
    x1_o[0] = x1
    h2 = x1 * (1.0 + m[4:5]) + m[3:4]
    h2_o[0] = h2.astype(h2_o.dtype)

    logits = _mm3(h2, wr_ref[...]) + br_ref[...]
    lane = lax.broadcasted_iota(jnp.int32, logits.shape, 1)
    work = logits
    sel = jnp.zeros(logits.shape, F32)
    top = None
    for _ in range(TOP_K):
        mx = jnp.max(work, -1, keepdims=True)
        first = jnp.min(jnp.where(work == mx, lane, LANES), -1, keepdims=True)
        pick = lane == first
        sel = jnp.where(pick, 1.0, sel)
        work = jnp.where(pick, -jnp.inf, work)
        top = mx if top is None else top
    e = jnp.where(sel > 0.0, jnp.exp(logits - top), 0.0)
    gates = e / jnp.sum(e, -1, keepdims=True)
    gate_o[0] = gates.T[:N_EXPERTS]


def _post_call(x, mod, in_g, in_b, wkv_f, wkv_b, bonus, g, y_b, gn_g, gn_b, bd, w_out, ln1_g, ln1_b,
               w_router_p, b_router_p):
    B, T, D = x.shape
    tm = min(256, T)
    const = lambda b, t: (0, 0)
    tile = lambda w: pl.BlockSpec((1, tm, w), lambda b, t: (b, t, 0))
    vec = lambda w: pl.BlockSpec((1, w), const)
    return pl.pallas_call(
        _post_kernel,
        grid=(B, T // tm),
        in_specs=[tile(D), pl.BlockSpec((1, 6, D), lambda b, t: (b, 0, 0)), vec(D), vec(D),
                  tile(D_RWKV), tile(D_RWKV), tile(D_RWKV), tile(D_RWKV), tile(D_CONV),
                  vec(D_RWKV), vec(D_RWKV), pl.BlockSpec(bd.shape, const),
                  pl.BlockSpec(w_out.shape, const), vec(D), vec(D),
                  pl.BlockSpec(w_router_p.shape, const), vec(LANES)],
        out_specs=[tile(D), tile(D), pl.BlockSpec((1, N_EXPERTS, tm), lambda b, t: (b, 0, t))],
        out_shape=[jax.ShapeDtypeStruct((B, T, D), F32), jax.ShapeDtypeStruct((B, T, D), BF16),
                   jax.ShapeDtypeStruct((B, N_EXPERTS, T), F32)],
        compiler_params=_params(("parallel", "parallel")),
        name="post",
    )(x, mod, _row(in_g), _row(in_b), wkv_f, wkv_b, bonus, g, y_b, _row(gn_g), _row(gn_b), bd, w_out,
      _row(ln1_g), _row(ln1_b), w_router_p, b_router_p)


def _routing(gates_t, tb):
    B, E, T = gates_t.shape
    nb = T // tb
    n_blocks = B * nb
    cnt = jnp.sum((gates_t > 0.0).reshape(B, E, nb, tb), -1, dtype=jnp.int32)
    cnt = cnt.transpose(1, 0, 2).reshape(E, n_blocks)
    cnt_al = (cnt + ROW_ALIGN - 1) // ROW_ALIGN * ROW_ALIGN
    n_e = jnp.sum(cnt_al, 1)
    seg = (n_e + CAP + FFN_TILE - 1) // FFN_TILE * FFN_TILE
    seg_end = jnp.cumsum(seg)
    seg_start = seg_end - seg
    start = seg_start[:, None] + jnp.cumsum(cnt_al, 1) - cnt_al
    z0 = seg_start + n_e // FFN_TILE * FFN_TILE
    zrow = jnp.concatenate([z0, jnp.minimum(z0 + FFN_TILE, seg_end[-1] - FFN_TILE), seg_end[-1:] // FFN_TILE])
    rows_max = TOP_K * B * T + (ROW_ALIGN - 1) * n_blocks * E + E * (CAP + FFN_TILE - 1)
    n_tiles = -(-rows_max // FFN_TILE)
    idx = jnp.arange(n_tiles, dtype=jnp.int32)
    n_valid = seg_end[-1] // FFN_TILE
    src = jnp.minimum(idx, n_valid - 1)
    tile_e = jnp.sum(seg_end[None, :] <= (src * FFN_TILE)[:, None], 1, dtype=jnp.int32)
    over = jnp.max(cnt, 0) > CAP
    return dict(start=start.reshape(-1), cnt=cnt.reshape(-1), over=over.astype(jnp.int32), zrow=zrow,
                tile_e=tile_e, src=src, valid=(idx < n_valid).astype(jnp.int32), n_tiles=n_tiles)


def _block_ranks(g, tri):
    routed = g > 0.0
    rank = jnp.dot(jnp.where(routed, 1.0, 0.0).astype(BF16), tri, preferred_element_type=F32)
    return jnp.where(routed, rank, -1.0)


def _group_copy(src, hbm, sem, row):
    return pltpu.make_async_copy(src, hbm.at[pl.ds(pl.multiple_of(row, ROW_ALIGN), CAP)], sem)


def _dispatch_kernel(start_ref, cnt_ref, over_ref, zrow_ref, h_ref, g_ref, tri_ref, xs_out, stage, xbuf, rank_s,
                     sem):
    E, tb = g_ref.shape[1], g_ref.shape[2]
    n_blocks = pl.num_programs(0) * pl.num_programs(1)
    blk = pl.program_id(0) * pl.num_programs(1) + pl.program_id(1)
    cur = blk % 2

    @pl.when(blk == 0)
    def _():
        zeros = stage.at[1, pl.ds(0, FFN_TILE)]

        def zero_copy(row):
            return pltpu.make_async_copy(zeros, xs_out.at[pl.ds(pl.multiple_of(row, FFN_TILE), FFN_TILE)], sem.at[1])

        stage[1, 0:FFN_TILE] = jnp.zeros((FFN_TILE, stage.shape[2]), stage.dtype)
        for i in range(E):
            zero_copy(zrow_ref[i]).start()
        for i in range(E):
            zero_copy(zrow_ref[i]).wait()
        for i in range(E, 2 * E):
            cp = zero_copy(zrow_ref[i])
            cp.start()
            cp.wait()
        first, last = zrow_ref[2 * E], xs_out.shape[0] // FFN_TILE
        lax.fori_loop(first, last, lambda i, c: (zero_copy(i * FFN_TILE).start(), c)[1], 0)
        lax.fori_loop(first, last, lambda i, c: (zero_copy(i * FFN_TILE).wait(), c)[1], 0)

    def group_copies(half, block):
        return [_group_copy(stage.at[half, pl.ds(e * CAP, CAP)], xs_out, sem.at[0], start_ref[e * n_blocks + block])
                for e in range(E)]

    rank = _block_ranks(g_ref[0], tri_ref[...])
    rank_s[...] = rank
    slot = lax.broadcasted_iota(jnp.int32, (CAP, tb), 0).astype(F32)
    h = h_ref[0]
    n_part = 4
    per = E // n_part
    for q in range(n_part):
        sel = jnp.concatenate([jnp.where(rank[e:e + 1] == slot, 1.0, 0.0).astype(BF16)
                               for e in range(q * per, (q + 1) * per)], axis=0)
        stage[cur, q * per * CAP:(q + 1) * per * CAP] = jnp.dot(sel, h, preferred_element_type=F32).astype(BF16)

    @pl.when(blk > 0)
    def _():
        for cp in group_copies(1 - cur, blk - 1):
            cp.wait()

    for cp in group_copies(cur, blk):
        cp.start()

    @pl.when(over_ref[blk] > 0)
    def _():
        def per_expert(e, carry):
            def per_chunk(c, carry):
                row = rank_s[pl.ds(e, 1), :] - (c * CAP).astype(F32)
                sel = jnp.where(row == slot, 1.0, 0.0).astype(BF16)
                xbuf[...] = jnp.dot(sel, h, preferred_element_type=F32).astype(BF16)
                cp = _group_copy(xbuf, xs_out, sem.at[1], start_ref[e * n_blocks + blk] + c * CAP)
                cp.start()
                cp.wait()
                return carry
            n_chunks = (cnt_ref[e * n_blocks + blk] + CAP - 1) // CAP
            return lax.fori_loop(1, n_chunks, per_chunk, carry)
        lax.fori_loop(0, E, per_expert, 0)

    @pl.when(blk == n_blocks - 1)
    def _():
        for cp in group_copies(cur, blk):
            cp.wait()


def _dispatch_call(h2, gates_t, tri, rt):
    B, T, D = h2.shape
    E = gates_t.shape[1]
    tb = tri.shape[0]
    rows = rt["n_tiles"] * FFN_TILE
    assert E * CAP >= FFN_TILE
    grid_spec = pltpu.PrefetchScalarGridSpec(
        num_scalar_prefetch=4,
        grid=(B, T // tb),
        in_specs=[pl.BlockSpec((1, tb, D), lambda b, t, *_: (b, t, 0)),
                  pl.BlockSpec((1, E, tb), lambda b, t, *_: (b, 0, t)),
                  pl.BlockSpec((tb, tb), lambda b, t, *_: (0, 0))],
        out_specs=pl.BlockSpec(memory_space=pl.ANY),
        scratch_shapes=[pltpu.VMEM((2, E * CAP, D), BF16), pltpu.VMEM((CAP, D), BF16), pltpu.VMEM((E, tb), F32),
                        pltpu.SemaphoreType.DMA((2,))],
    )
    return pl.pallas_call(
        _dispatch_kernel, grid_spec=grid_spec,
        out_shape=jax.ShapeDtypeStruct((rows, D), BF16),
        compiler_params=_params(("arbitrary", "arbitrary")),
        name="dispatch",
    )(rt["start"], rt["cnt"], rt["over"], rt["zrow"], h2, gates_t, tri)


def _ffn_kernel(te_ref, src_ref, valid_ref, x_ref, wgu_ref, bgu_ref, wd_ref, bdn_ref, o_ref):
    i = pl.program_id(0)

    @pl.when(valid_ref[i] > 0)
    def _():
        h = jnp.dot(x_ref[...], wgu_ref[0], preferred_element_type=F32) + bgu_ref[0]
        h_glu = jnp.minimum(h[:, :D_FF], SWIGLU_LIMIT)
        h_lin = jnp.clip(h[:, D_FF:], -SWIGLU_LIMIT, SWIGLU_LIMIT)
        y = (h_lin + 1.0) * (h_glu * _sigmoid(SWIGLU_ALPHA * h_glu))
        o_ref[...] = (_mm(y, wd_ref[0]) + bdn_ref[0]).astype(o_ref.dtype)

    @pl.when(valid_ref[i] == 0)
    def _():
        o_ref[...] = jnp.zeros_like(o_ref)


def _ffn_call(xs, rt, w_gu, b_gu, w_down, b_down):
    rows, D = xs.shape
    E = w_gu.shape[0]
    grid_spec = pltpu.PrefetchScalarGridSpec(
        num_scalar_prefetch=3,
        grid=(rt["n_tiles"],),
        in_specs=[pl.BlockSpec((FFN_TILE, D), lambda i, te, src, valid: (src[i], 0)),
                  pl.BlockSpec((1, D, 2 * D_FF), lambda i, te, src, valid: (te[i], 0, 0)),
                  pl.BlockSpec((1, 1, 2 * D_FF), lambda i, te, src, valid: (te[i], 0, 0)),
                  pl.BlockSpec((1, D_FF, D), lambda i, te, src, valid: (te[i], 0, 0)),
                  pl.BlockSpec((1, 1, D), lambda i, te, src, valid: (te[i], 0, 0))],
        out_specs=pl.BlockSpec((FFN_TILE, D), lambda i, te, src, valid: (i, 0)),
    )
    return pl.pallas_call(
        _ffn_kernel, grid_spec=grid_spec,
        out_shape=jax.ShapeDtypeStruct((rows, D), BF16),
        compiler_params=_params(("arbitrary",)),
        name="ffn",
    )(rt["tile_e"], rt["src"], rt["valid"], xs, w_gu, b_gu.reshape(E, 1, -1), w_down,
      b_down.reshape(E, 1, -1))


def _combine_kernel(start_ref, cnt_ref, over_ref, g_ref, x1_ref, mod_ref, tri_ref, lg_ref, lb_ref, ys_ref, o_ref,
                    ybuf, xbuf, rank_s, acc_s, sem):
    E, tb = g_ref.shape[1], g_ref.shape[2]
    n_blocks = pl.num_programs(0) * pl.num_programs(1)
    blk = pl.program_id(0) * pl.num_programs(1) + pl.program_id(1)
    cur = blk % 2

    def fetch(half, e, block):
        row = start_ref[e * n_blocks + block]
        return pltpu.make_async_copy(ys_ref.at[pl.ds(pl.multiple_of(row, ROW_ALIGN), CAP)],
                                     ybuf.at[half, pl.ds(e * CAP, CAP)], sem.at[half])

    @pl.when(blk == 0)
    def _():
        for e in range(E):
            fetch(0, e, 0).start()

    @pl.when(blk + 1 < n_blocks)
    def _():
        for e in range(E):
            fetch(1 - cur, e, blk + 1).start()

    g = g_ref[0]
    rank = _block_ranks(g, tri_ref[...])
    slot = lax.broadcasted_iota(jnp.int32, (CAP, tb), 0).astype(F32)
    for e in range(E):
        fetch(cur, e, blk).wait()
    n_part = 4
    per = E // n_part
    acc = None
    for q in range(n_part):
        w = jnp.concatenate([jnp.where(rank[e:e + 1] == slot, g[e:e + 1], 0.0).astype(BF16)
                             for e in range(q * per, (q + 1) * per)], axis=0)
        part = _mm_tn(w, ybuf[cur, q * per * CAP:(q + 1) * per * CAP])
        acc = part if acc is None else acc + part
    acc_s[...] = acc

    @pl.when(over_ref[blk] > 0)
    def _():
        rank_s[...] = rank

        def per_expert(e, carry):
            def per_chunk(c, carry):
                row0 = start_ref[e * n_blocks + blk] + c * CAP
                cp = pltpu.make_async_copy(ys_ref.at[pl.ds(pl.multiple_of(row0, ROW_ALIGN), CAP)], xbuf, sem.at[2])
                cp.start()
                row = rank_s[pl.ds(e, 1), :] - (c * CAP).astype(F32)
                w = jnp.where(row == slot, g_ref[0, pl.ds(e, 1), :], 0.0).astype(BF16)
                cp.wait()
                acc_s[...] += _mm_tn(w, xbuf[...])
                return carry
            n_chunks = (cnt_ref[e * n_blocks + blk] + CAP - 1) // CAP
            return lax.fori_loop(1, n_chunks, per_chunk, carry)
        lax.fori_loop(0, E, per_expert, 0)

    m = mod_ref[0]
    o_ref[0] = _ln(DEEPNORM_ALPHA * x1_ref[0] + (1.0 + m[5:6]) * acc_s[...], lg_ref[...], lb_ref[...], LN_EPS)


def _combine_call(ys, gates_t, x1, mod, tri, rt, ln2_g, ln2_b):
    B, T, D = x1.shape
    E = gates_t.shape[1]
    tb = tri.shape[0]
    const = lambda b, t, *_: (0, 0)
    grid_spec = pltpu.PrefetchScalarGridSpec(
        num_scalar_prefetch=3,
        grid=(B, T // tb),
        in_specs=[pl.BlockSpec((1, E, tb), lambda b, t, *_: (b, 0, t)),
                  pl.BlockSpec((1, tb, D), lambda b, t, *_: (b, t, 0)),
                  pl.BlockSpec((1, 6, D), lambda b, t, *_: (b, 0, 0)),
                  pl.BlockSpec((tb, tb), const),
                  pl.BlockSpec((1, D), const), pl.BlockSpec((1, D), const),
                  pl.BlockSpec(memory_space=pl.ANY)],
        out_specs=pl.BlockSpec((1, tb, D), lambda b, t, *_: (b, t, 0)),
        scratch_shapes=[pltpu.VMEM((2, E * CAP, D), BF16), pltpu.VMEM((CAP, D), BF16), pltpu.VMEM((E, tb), F32),
                        pltpu.VMEM((tb, D), F32), pltpu.SemaphoreType.DMA((3,))],
    )
    return pl.pallas_call(
        _combine_kernel, grid_spec=grid_spec,
        out_shape=jax.ShapeDtypeStruct((B, T, D), F32),
        compiler_params=_params(("arbitrary", "arbitrary")),
        name="combine",
    )(rt["start"], rt["cnt"], rt["over"], gates_t, x1, mod, tri, _row(ln2_g), _row(ln2_b), ys)


def _moe_call(h2, gates_t, x1, mod, w_gu, b_gu, w_down, b_down, ln2_g, ln2_b):
    tb = min(TOKEN_BLOCK, h2.shape[1])
    pos = lax.broadcasted_iota(jnp.int32, (tb, tb), 0)
    tri = (pos < pos.T).astype(BF16)
    rt = _routing(gates_t, tb)
    xs = _dispatch_call(h2, gates_t, tri, rt)
    ys = _ffn_call(xs, rt, w_gu, b_gu, w_down, b_down)
    return _combine_call(ys, gates_t, x1, mod, tri, rt, ln2_g, ln2_b)


def _block_diag_cat(w):
    z = jnp.zeros_like(w[0])
    return jnp.concatenate([jnp.concatenate([w[0], z], 1), jnp.concatenate([z, w[1]], 1)], 0)


def _encode(x, mod, W):
    p_r, p_c = _inproj_call(x, mod, W["in_g"], W["in_b"], W["w_in_r"], W["w_in_c"])
    r, v, kk, lw, kd, bb, g, bonus = _pre_call(p_r, W["mu_shift"], W["w0c"], W["w2c"], W["a0c"], W["a2c"],
                                               W["g_up"], W["k_k"], W["k_a"], W["r_k"], W["bd"])
    wkv_f, wkv_b = _scan_call(r, v, kk, lw, kd, bb)
    y_b = _conv_call(p_c, W["dw_w"], W["dw_b"], W["cln_g"], W["cln_b"])
    x1, h2, gates = _post_call(x, mod, W["in_g"], W["in_b"], wkv_f, wkv_b, bonus, g, y_b, W["gn_g"], W["gn_b"],
                               W["bd"], W["w_out"], W["ln1_g"], W["ln1_b"], W["w_router_p"], W["b_router_p"])
    return _moe_call(h2, gates, x1, mod, W["w_gu"], W["b_gu"], W["w_down"], W["b_down"], W["ln2_g"], W["ln2_b"])


def kernel(x_prompt, x_sample, c_prompt, c_sample, in_g, in_b, w_mod, b_mod, w_in, mu_shift, w0, w2, a0, a2, g_up, k_k, k_a, r_k, gn_g, gn_b, dw_w, dw_b, cln_g, cln_b, w_out, ln1_g, ln1_b, w_router, b_router, w_gu, b_gu, w_down, b_down, ln2_g, ln2_b):
    l = 0
    head = lax.broadcasted_iota(jnp.int32, (D_RWKV, D_RWKV), 0) // HEAD_DIM
    W = dict(
        in_g=in_g, in_b=in_b,
        w_in_r=w_in[l][:, :N_SHIFT_COLS].astype(BF16), w_in_c=w_in[l][:, N_SHIFT_COLS:].astype(BF16),
        mu_shift=mu_shift[l],
        w0c=w0[l].reshape(1, -1), w2c=_block_diag_cat(w2[l]).astype(BF16),
        a0c=a0[l].reshape(1, -1), a2c=_block_diag_cat(a2[l]).astype(BF16),
        g_up=g_up[l].astype(BF16), k_k=_row(k_k[l]), k_a=_row(k_a[l]), r_k=_row(r_k[l]),
        bd=(head == head.T).astype(BF16),
        gn_g=gn_g[l], gn_b=gn_b[l], dw_w=dw_w[l], dw_b=dw_b[l], cln_g=cln_g[l], cln_b=cln_b[l],
        w_out=w_out[l].astype(BF16), ln1_g=ln1_g[l], ln1_b=ln1_b[l],
        w_router_p=jnp.pad(w_router[l], ((0, 0), (0, LANES - N_EXPERTS))),
        b_router_p=jnp.pad(b_router[l], (0, LANES - N_EXPERTS), constant_values=-1e30).reshape(1, -1),
        w_gu=w_gu[l].astype(BF16), b_gu=b_gu[l], w_down=w_down[l].astype(BF16), b_down=b_down[l],
        ln2_g=ln2_g[l], ln2_b=ln2_b[l],
    )
    nb = x_prompt.shape[0]
    mod = _mod_call(jnp.concatenate([c_prompt, c_sample], 0), w_mod[l], b_mod[l])
    return (_encode(x_prompt, mod[:nb], W), _encode(x_sample, mod[nb:], W))
```

```python
import functools

import jax
import jax.numpy as jnp
from jax import lax
from jax.experimental import pallas as pl
from jax.experimental.pallas import tpu as pltpu

F32 = jnp.float32
BF16 = jnp.bfloat16

D_MODEL = 1024
D_RWKV = 512
HEAD_DIM = 64
N_HEADS = 8
D_CONV = 512
DECAY_RANK = 64
AAA_RANK = 64
GATE_RANK = 128
CONV_WIDTH = 31
N_EXPERTS = 32
TOP_K = 4
D_FF = 1024
SWIGLU_LIMIT = 7.0
SWIGLU_ALPHA = 1.702
LN_EPS = 1e-5
GN_EPS = 64e-5
DEPTH = 1
DEEPNORM_ALPHA = (2.0 * DEPTH) ** 0.25
N_SHIFT_COLS = 3 * D_RWKV + 2 * DECAY_RANK + 2 * AAA_RANK + GATE_RANK

LANES = 128
SUBLANES = 8
HALO = 16
CHUNK = 64
SCAN_BLOCK = 8 * CHUNK
TOKEN_BLOCK = 512
CAP = 128
ROW_ALIGN = 16
FFN_TILE = 1024
GROUP = 4
GROUP_W = GROUP * HEAD_DIM
VMEM_LIMIT = 56 * 1024 * 1024


def _params(sem, flags=None):
    return pltpu.CompilerParams(dimension_semantics=sem, vmem_limit_bytes=VMEM_LIMIT, flags=flags)


def _sigmoid(x):
    return 1.0 / (1.0 + jnp.exp(-x))


def _mm(a, b):
    return jnp.dot(a.astype(BF16), b.astype(BF16), preferred_element_type=F32)


def _mm_nt(a, b):
    return lax.dot_general(a.astype(BF16), b.astype(BF16), (((1,), (1,)), ((), ())),
                           preferred_element_type=F32)


def _mm_tn(a, b):
    return lax.dot_general(a.astype(BF16), b.astype(BF16), (((0,), (0,)), ((), ())),
                           preferred_element_type=F32)


def _split(a, n):
    parts = []
    for _ in range(n):
        h = a.astype(BF16)
        parts.append(h)
        a = a - h.astype(F32)
    return parts


def _mm_exact_rhs(a, b_bf16, n):
    acc = None
    for h in _split(a, n):
        t = jnp.dot(h, b_bf16, preferred_element_type=F32)
        acc = t if acc is None else acc + t
    return acc


def _mm_exact_lhs(a_bf16, b, n):
    acc = None
    for h in _split(b, n):
        t = jnp.dot(a_bf16, h, preferred_element_type=F32)
        acc = t if acc is None else acc + t
    return acc


def _mm3(a, b):
    ah, al = _split(a, 2)
    bh, bl = _split(b, 2)
    return (jnp.dot(ah, bh, preferred_element_type=F32) + jnp.dot(al, bh, preferred_element_type=F32)
            + jnp.dot(ah, bl, preferred_element_type=F32))


def _ln(x, g, b, eps):
    mu = jnp.mean(x, -1, keepdims=True)
    xc = x - mu
    var = jnp.mean(xc * xc, -1, keepdims=True)
    return xc * lax.rsqrt(var + eps) * g + b


def _row(a):
    return a.reshape(1, -1)


def _mod_kernel(c_ref, w_ref, b_ref, o_ref):
    c = c_ref[...]
    o_ref[...] = _mm3(c * _sigmoid(c), w_ref[...]) + b_ref[...]


def _mod_call(c, w_mod, b_mod):
    n, d = c.shape
    npad = -(-n // 8) * 8
    cp = jnp.pad(c, ((0, npad - n), (0, 0)))
    tn = 1536
    out = pl.pallas_call(
        _mod_kernel,
        grid=(6 * d // tn,),
        in_specs=[pl.BlockSpec((npad, d), lambda j: (0, 0)),
                  pl.BlockSpec((d, tn), lambda j: (0, j)),
                  pl.BlockSpec((1, tn), lambda j: (0, j))],
        out_specs=pl.BlockSpec((npad, tn), lambda j: (0, j)),
        out_shape=jax.ShapeDtypeStruct((npad, 6 * d), F32),
        compiler_params=_params(("arbitrary",)),
        name="mod",
    )(cp, w_mod, _row(b_mod))
    return out[:n].reshape(n, 6, d)


def _inproj_kernel(x_ref, mod_ref, g_ref, b_ref, wr_ref, wc_ref, pr_ref, pc_ref):
    x0 = _ln(x_ref[0], g_ref[...], b_ref[...], LN_EPS)
    m = mod_ref[0]
    h = (x0 * (1.0 + m[1:2]) + m[0:1]).astype(BF16)
    pr_ref[0] = jnp.dot(h, wr_ref[...], preferred_element_type=F32).astype(pr_ref.dtype)
    pc_ref[0] = jnp.dot(h, wc_ref[...], preferred_element_type=F32).astype(pc_ref.dtype)


def _inproj_call(x, mod, in_g, in_b, w_r, w_c):
    B, T, D = x.shape
    tm = min(512, T)
    const = lambda b, t: (0, 0)
    return pl.pallas_call(
        _inproj_kernel,
        grid=(B, T // tm),
        in_specs=[pl.BlockSpec((1, tm, D), lambda b, t: (b, t, 0)),
                  pl.BlockSpec((1, 6, D), lambda b, t: (b, 0, 0)),
                  pl.BlockSpec((1, D), const), pl.BlockSpec((1, D), const),
                  pl.BlockSpec(w_r.shape, const), pl.BlockSpec(w_c.shape, const)],
        out_specs=[pl.BlockSpec((1, tm, N_SHIFT_COLS), lambda b, t: (b, t, 0)),
                   pl.BlockSpec((1, tm, 2 * D_CONV), lambda b, t: (b, t, 0))],
        out_shape=[jax.ShapeDtypeStruct((B, T, N_SHIFT_COLS), BF16),
                   jax.ShapeDtypeStruct((B, T, 2 * D_CONV), BF16)],
        compiler_params=_params(("parallel", "parallel")),
        name="inproj",
    )(x, mod, _row(in_g), _row(in_b), w_r, w_c)


def _pre_kernel(p_ref, pp_ref, pn_ref, mu_ref, w0_ref, w2_ref, a0_ref, a2_ref, gup_ref, kkw_ref, ka_ref,
                rk_ref, bd_ref, r_o, v_o, kk_o, lw_o, kd_o, bb_o, g_o, bonus_o):
    t = pl.program_id(1)
    nt = pl.num_programs(1)
    p = p_ref[0].astype(F32)
    tm = p.shape[0]
    prev_row = jnp.where(t > 0, pp_ref[0].astype(F32)[HALO - 1:HALO], 0.0)
    next_row = jnp.where(t < nt - 1, pn_ref[0].astype(F32)[0:1], 0.0)
    row = lax.broadcasted_iota(jnp.int32, p.shape, 0)
    prev = jnp.where(row == 0, prev_row, pltpu.roll(p, 1, 0))
    nxt = jnp.where(row == tm - 1, next_row, pltpu.roll(p, tm - 1, 0))
    mu = mu_ref[...]
    ps = p + mu[0:1] * (prev - p) + mu[1:2] * (nxt - p)

    r = ps[:, 0:D_RWKV]
    k = ps[:, D_RWKV:2 * D_RWKV]
    v = ps[:, 2 * D_RWKV:3 * D_RWKV]
    o = 3 * D_RWKV
    wl = ps[:, o:o + 2 * DECAY_RANK]
    al = ps[:, o + 2 * DECAY_RANK:o + 2 * DECAY_RANK + 2 * AAA_RANK]
    gl = ps[:, o + 2 * DECAY_RANK + 2 * AAA_RANK:]

    y = -(w0_ref[...] + _mm(jnp.tanh(wl), w2_ref[...]))
    softplus = jnp.maximum(y, 0.0) + jnp.log(1.0 + jnp.exp(-jnp.abs(y)))
    lw = -jnp.exp(-softplus - 0.5)
    a = _sigmoid(a0_ref[...] + _mm(al, a2_ref[...]))
    g = _mm(_sigmoid(gl), gup_ref[...])

    bd = bd_ref[...]
    kk = k * kkw_ref[...]
    ss = _mm_exact_rhs(kk * kk, bd, 1)
    kk = kk * lax.rsqrt(jnp.maximum(ss, 1e-24))
    ka = ka_ref[...]
    a0d, a1d = a[:, :D_RWKV], a[:, D_RWKV:]
    k0 = k * (1.0 + (a0d - 1.0) * ka)
    k1 = k * (1.0 + (a1d - 1.0) * ka)
    bonus = _mm_exact_rhs(r * (k0 + k1) * rk_ref[...], bd, 1) * v

    r_o[0] = r.astype(r_o.dtype)
    v_o[0] = v.astype(v_o.dtype)
    kk_o[0] = kk.astype(kk_o.dtype)
    lw_o[0] = lw
    kd_o[0, :, :D_RWKV] = k0.astype(kd_o.dtype)
    kd_o[0, :, D_RWKV:] = k1.astype(kd_o.dtype)
    bb_o[0, :, :D_RWKV] = (kk * a0d).astype(bb_o.dtype)
    bb_o[0, :, D_RWKV:] = (kk * a1d).astype(bb_o.dtype)
    g_o[0] = g.astype(g_o.dtype)
    bonus_o[0] = bonus.astype(bonus_o.dtype)


def _pre_call(p_r, mu_shift, w0c, w2c, a0c, a2c, g_up, k_k, k_a, r_k, bd):
    B, T, C = p_r.shape
    tm = min(256, T)
    nh = tm // HALO
    const = lambda b, t: (0, 0)
    tile = lambda w: pl.BlockSpec((1, tm, w), lambda b, t: (b, t, 0))
    outs = [D_RWKV, D_RWKV, D_RWKV, 2 * D_RWKV, 2 * D_RWKV, 2 * D_RWKV, D_RWKV, D_RWKV]
    return pl.pallas_call(
        _pre_kernel,
        grid=(B, T // tm),
        in_specs=[tile(C),
                  pl.BlockSpec((1, HALO, C), lambda b, t: (b, jnp.maximum(t * nh - 1, 0), 0)),
                  pl.BlockSpec((1, HALO, C), lambda b, t: (b, jnp.minimum((t + 1) * nh, T // HALO - 1), 0)),
                  pl.BlockSpec(mu_shift.shape, const), pl.BlockSpec(w0c.shape, const),
                  pl.BlockSpec(w2c.shape, const), pl.BlockSpec(a0c.shape, const),
                  pl.BlockSpec(a2c.shape, const), pl.BlockSpec(g_up.shape, const),
                  pl.BlockSpec(k_k.shape, const), pl.BlockSpec(k_a.shape, const),
                  pl.BlockSpec(r_k.shape, const), pl.BlockSpec(bd.shape, const)],
        out_specs=[tile(w) for w in outs],
        out_shape=[jax.ShapeDtypeStruct((B, T, w), F32 if i == 3 else BF16) for i, w in enumerate(outs)],
        compiler_params=_params(("parallel", "parallel")),
        name="pre",
    )(p_r, p_r, p_r, mu_shift, w0c, w2c, a0c, a2c, g_up, k_k, k_a, r_k, bd)


def _chains_chunk(chains):
    assert CHUNK == HEAD_DIM
    L = chains[0]["rt"].shape[0]
    R = GROUP * L
    rowh = lax.broadcasted_iota(jnp.int32, (R, GROUP_W), 0) // L
    colh = lax.broadcasted_iota(jnp.int32, (R, GROUP_W), 1) // HEAD_DIM
    own = rowh == colh

    def bd(x):
        return jnp.where(own, jnp.concatenate([x.astype(BF16)] * GROUP, axis=0), jnp.zeros((), BF16))

    def dot(a, b):
        return jnp.dot(a.astype(BF16), b, preferred_element_type=F32)

    t_row = lax.broadcasted_iota(jnp.int32, (L, GROUP_W), 0)
    i_col = lax.broadcasted_iota(jnp.int32, (L, GROUP_W), 1) % L
    eye = jnp.where(t_row == i_col, 1.0, 0.0)
    masks = {False: (i_col < t_row, i_col <= t_row), True: (i_col > t_row, i_col >= t_row)}

    zr = [jnp.concatenate([c["zt"], c["rt"]], axis=0).astype(BF16) for c in chains]
    a_b = [_mm_nt(x, bd(c["bt"])) for x, c in zip(zr, chains)]
    a_k = [_mm_nt(x, bd(c["kt"])) for x, c in zip(zr, chains)]
    n_zb = [jnp.where(masks[c["reverse"]][0], a[:L], 0.0) for c, a in zip(chains, a_b)]
    a_rb = [jnp.where(masks[c["reverse"]][1], a[L:], 0.0) for c, a in zip(chains, a_b)]
    a_zr = [jnp.concatenate([jnp.where(masks[c["reverse"]][0], a[:L], 0.0),
                             jnp.where(masks[c["reverse"]][1], a[L:], 0.0)], axis=0) for c, a in zip(chains, a_k)]

    tinv = [eye + n for n in n_zb]
    pw = [dot(n, bd(n)) for n in n_zb]
    levels = L.bit_length() - 2
    for j in range(1, levels + 1):
        if j < levels:
            both = [dot(jnp.concatenate([p, t], axis=0), bd(p)) for p, t in zip(pw, tinv)]
            pw = [b[:L] for b in both]
            tinv = [t + b[L:] for t, b in zip(tinv, both)]
        else:
            tinv = [t + dot(t, bd(p)) for p, t in zip(pw, tinv)]

    av = [dot(a, bd(c["v"])) for a, c in zip(a_zr, chains)]

    def rows(x):
        return jnp.concatenate([x[:, h * HEAD_DIM:(h + 1) * HEAD_DIM] for h in range(GROUP)], axis=0)

    def advance(idx, S):
        zs = [_mm_nt(zr[i], bd(s)) for i, s in zip(idx, S)]
        u = [dot(tinv[i], bd(z[:L] + av[i][:L])) for i, z in zip(idx, zs)]
        outs = [z[L:] + av[i][L:] + dot(a_rb[i], bd(x)) for i, z, x in zip(idx, zs, u)]
        nxt = [s * chains[i]["wtot"] + _mm_tn(rows(x), bd(chains[i]["bhat"]))
               + _mm_tn(rows(chains[i]["v"]), bd(chains[i]["khat"])) for i, s, x in zip(idx, S, u)]
        return outs, nxt

    return advance


def _prep_chains(r, v, kk, lw, kd, bb, reverse):
    L = r.shape[0]
    r, kk, kd, bb = (x.astype(F32) for x in (r, kk, kd, bb))
    ti = lax.broadcasted_iota(jnp.int32, (L, L), 0)
    ii = lax.broadcasted_iota(jnp.int32, (L, L), 1)
    tri = jnp.where((ii >= ti) if reverse else (ii <= ti), 1.0, 0.0).astype(BF16)
    cum = _mm_exact_lhs(tri, lw, 3)
    tot = cum[0:1] if reverse else cum[L - 1:L]
    ec, eci, ecx, eto = jnp.exp(cum), jnp.exp(-cum), jnp.exp(cum - lw), jnp.exp(tot - cum)
    full = dict(rt=r * ec, zt=-(kk * ecx), kt=kd * eci, bt=bb * eci, khat=kd * eto, bhat=bb * eto, v=v,
                wtot=jnp.exp(tot))
    chains = []
    for g in range(D_RWKV // GROUP_W):
        c = {k: x[:, g * GROUP_W:(g + 1) * GROUP_W] for k, x in full.items()}
        c["reverse"] = reverse
        chains.append(c)
    return chains


def _scan_kernel(rf, vf, kkf, lwf, kdf, bbf, rb, vb, kkb, lwb, kdb, bbb, of, ob, s_ref):
    @pl.when(pl.program_id(1) == 0)
    def _():
        s_ref[...] = jnp.zeros_like(s_ref)

    n_sub = rf.shape[1] // CHUNK
    n_grp = D_RWKV // GROUP_W
    per = 2 * n_grp
    chains, spans = [], []
    for i in range(n_sub):
        lf = pl.ds(i * CHUNK, CHUNK)
        lb = pl.ds((n_sub - 1 - i) * CHUNK, CHUNK)
        spans.append((lf, lb))
        chains += _prep_chains(rf[0, lf], vf[0, lf], kkf[0, lf], lwf[0, lf], kdf[0, lf], bbf[0, lf], False)
        chains += _prep_chains(rb[0, lb], vb[0, lb], kkb[0, lb], lwb[0, lb], kdb[0, lb], bbb[0, lb], True)
    advance = _chains_chunk(chains)
    sf, sb = s_ref[0], s_ref[1]
    states = ([sf[:, g * GROUP_W:(g + 1) * GROUP_W] for g in range(n_grp)]
              + [sb[:, g * GROUP_W:(g + 1) * GROUP_W] for g in range(n_grp)])
    for i, (lf, lb) in enumerate(spans):
        outs, states = advance(list(range(i * per, (i + 1) * per)), states)
        of[0, lf] = jnp.concatenate(outs[:n_grp], axis=1)
        ob[0, lb] = jnp.concatenate(outs[n_grp:], axis=1)
    s_ref[0] = jnp.concatenate(states[:n_grp], axis=1)
    s_ref[1] = jnp.concatenate(states[n_grp:], axis=1)


def _scan_call(r, v, kk, lw, kd, bb):
    B, T, C = r.shape
    L = min(SCAN_BLOCK, T)
    nc = T // L
    fwd = lambda j: pl.BlockSpec((1, L, C), lambda b, c: (b, c, j))
    bwd = lambda j: pl.BlockSpec((1, L, C), lambda b, c: (b, nc - 1 - c, j))
    return pl.pallas_call(
        _scan_kernel,
        grid=(B, nc),
        in_specs=[fwd(0)] * 6 + [bwd(0)] * 3 + [bwd(1)] * 3,
        out_specs=[fwd(0), bwd(0)],
        out_shape=[jax.ShapeDtypeStruct((B, T, C), F32)] * 2,
        scratch_shapes=[pltpu.VMEM((2, HEAD_DIM, C), F32)],
        compiler_params=_params(("parallel", "arbitrary")),
        name="scan",
    )(r, v, kk, lw, kd, bb, r, v, kk, lw, kd, bb)


def _conv_kernel(p_ref, pp_ref, pn_ref, w_ref, b_ref, g_ref, be_ref, o_ref, h_ref, hs_ref):
    t = pl.program_id(1)
    nt = pl.num_programs(1)
    tm = p_ref.shape[1]

    def glu(x):
        x = x.astype(F32)
        return x[:, :D_CONV] * _sigmoid(x[:, D_CONV:])

    h_ref[0:HALO] = jnp.where(t > 0, glu(pp_ref[0]), 0.0)
    h_ref[HALO:HALO + tm] = glu(p_ref[0])
    h_ref[HALO + tm:] = jnp.where(t < nt - 1, glu(pn_ref[0]), 0.0)
    rows = hs_ref.shape[1]
    for s in range(1, SUBLANES):
        hs_ref[s - 1] = h_ref[s:s + rows]
    w = w_ref[...]
    acc = jnp.zeros((tm, D_CONV), F32) + b_ref[...]
    base = HALO - CONV_WIDTH // 2
    for j in range(CONV_WIDTH):
        off, s = (base + j) // SUBLANES * SUBLANES, (base + j) % SUBLANES
        tap = h_ref[off:off + tm] if s == 0 else hs_ref[s - 1, off:off + tm]
        acc = acc + w[j:j + 1] * tap
    y = _ln(acc, g_ref[...], be_ref[...], LN_EPS)
    o_ref[0] = (y * _sigmoid(y)).astype(o_ref.dtype)


def _conv_call(p_c, dw_w, dw_b, cln_g, cln_b):
    B, T, C = p_c.shape
    tm = min(256, T)
    nh = tm // HALO
    const = lambda b, t: (0, 0)
    return pl.pallas_call(
        _conv_kernel,
        grid=(B, T // tm),
        in_specs=[pl.BlockSpec((1, tm, C), lambda b, t: (b, t, 0)),
                  pl.BlockSpec((1, HALO, C), lambda b, t: (b, jnp.maximum(t * nh - 1, 0), 0)),
                  pl.BlockSpec((1, HALO, C), lambda b, t: (b, jnp.minimum((t + 1) * nh, T // HALO - 1), 0)),
                  pl.BlockSpec(dw_w.shape, const), pl.BlockSpec((1, D_CONV), const),
                  pl.BlockSpec((1, D_CONV), const), pl.BlockSpec((1, D_CONV), const)],
        out_specs=pl.BlockSpec((1, tm, D_CONV), lambda b, t: (b, t, 0)),
        out_shape=jax.ShapeDtypeStruct((B, T, D_CONV), BF16),
        scratch_shapes=[pltpu.VMEM((tm + 2 * HALO, D_CONV), F32),
                        pltpu.VMEM((SUBLANES - 1, tm + 2 * HALO - SUBLANES, D_CONV), F32)],
        compiler_params=_params(("parallel", "parallel")),
        name="conv",
    )(p_c, p_c, p_c, dw_w, _row(dw_b), _row(cln_g), _row(cln_b))


def _post_kernel(x_ref, mod_ref, ing_ref, inb_ref, wf_ref, wb_ref, bonus_ref, g_ref, yb_ref, gng_ref, gnb_ref,
                 bd_ref, wo_ref, l1g_ref, l1b_ref, wr_ref, br_ref, x1_o, h2_o, gate_o):
    m = mod_ref[0]
    x0 = _ln(x_ref[0], ing_ref[...], inb_ref[...], LN_EPS)
    bd = bd_ref[...]
    wkv = wf_ref[0] + wb_ref[0]
    mu = _mm_exact_rhs(wkv, bd, 2) * (1.0 / HEAD_DIM)
    wc = wkv - mu
    var = _mm_exact_rhs(wc * wc, bd, 1) * (1.0 / HEAD_DIM)
    o = wc * lax.rsqrt(var + GN_EPS) * gng_ref[...] + gnb_ref[...]
    ya = (o + bonus_ref[0].astype(F32)) * g_ref[0].astype(F32)
    wo = wo_ref[...]
    mix = _mm(ya, wo[:D_RWKV]) + jnp.dot(yb_ref[0], wo[D_RWKV:], preferred_element_type=F32)
    x1 = _ln(DEEPNORM_ALPHA * x0 + (1.0 + m[2:3]) * mix, l1g_ref[...], l1b_ref[...], LN_EPS)
    x1_o[0] = x1
    h2 = x1 * (1.0 + m[4:5]) + m[3:4]
    h2_o[0] = h2.astype(h2_o.dtype)

    logits = _mm3(h2, wr_ref[...]) + br_ref[...]
    lane = lax.broadcasted_iota(jnp.int32, logits.shape, 1).astype(F32)
    work = logits
    sel = jnp.zeros(logits.shape, F32)
    top = None
    for _ in range(TOP_K):
        mx = jnp.max(work, -1, keepdims=True)
        first = jnp.min(jnp.where(work == mx, lane, float(LANES)), -1, keepdims=True)
        pick = lane == first
        sel = jnp.where(pick, 1.0, sel)
        work = jnp.where(pick, -jnp.inf, work)
        top = mx if top is None else top
    e = jnp.where(sel > 0.0, jnp.exp(logits - top), 0.0)
    gates = e / jnp.sum(e, -1, keepdims=True)
    gate_o[0] = gates.T[:N_EXPERTS]


def _post_call(x, mod, in_g, in_b, wkv_f, wkv_b, bonus, g, y_b, gn_g, gn_b, bd, w_out, ln1_g, ln1_b,
               w_router_p, b_router_p):
    B, T, D = x.shape
    tm = min(256, T)
    const = lambda b, t: (0, 0)
    tile = lambda w: pl.BlockSpec((1, tm, w), lambda b, t: (b, t, 0))
    vec = lambda w: pl.BlockSpec((1, w), const)
    return pl.pallas_call(
        _post_kernel,
        grid=(B, T // tm),
        in_specs=[tile(D), pl.BlockSpec((1, 6, D), lambda b, t: (b, 0, 0)), vec(D), vec(D),
                  tile(D_RWKV), tile(D_RWKV), tile(D_RWKV), tile(D_RWKV), tile(D_CONV),
                  vec(D_RWKV), vec(D_RWKV), pl.BlockSpec(bd.shape, const),
                  pl.BlockSpec(w_out.shape, const), vec(D), vec(D),
                  pl.BlockSpec(w_router_p.shape, const), vec(LANES)],
        out_specs=[tile(D), tile(D), pl.BlockSpec((1, N_EXPERTS, tm), lambda b, t: (b, 0, t))],
        out_shape=[jax.ShapeDtypeStruct((B, T, D), F32), jax.ShapeDtypeStruct((B, T, D), BF16),
                   jax.ShapeDtypeStruct((B, N_EXPERTS, T), F32)],
        compiler_params=_params(("parallel", "parallel")),
        name="post",
    )(x, mod, _row(in_g), _row(in_b), wkv_f, wkv_b, bonus, g, y_b, _row(gn_g), _row(gn_b), bd, w_out,
      _row(ln1_g), _row(ln1_b), w_router_p, b_router_p)


def _routing(gates_t, tb):
    B, E, T = gates_t.shape
    nb = T // tb
    n_blocks = B * nb
    cnt = jnp.sum((gates_t > 0.0).reshape(B, E, nb, tb), -1, dtype=jnp.int32)
    cnt = cnt.transpose(1, 0, 2).reshape(E, n_blocks)
    cnt_al = (cnt + ROW_ALIGN - 1) // ROW_ALIGN * ROW_ALIGN
    n_e = jnp.sum(cnt_al, 1)
    seg = (n_e + CAP + FFN_TILE - 1) // FFN_TILE * FFN_TILE
    seg_end = jnp.cumsum(seg)
    seg_start = seg_end - seg
    start = seg_start[:, None] + jnp.cumsum(cnt_al, 1) - cnt_al
    z0 = seg_start + n_e // FFN_TILE * FFN_TILE
    zrow = jnp.concatenate([z0, jnp.minimum(z0 + FFN_TILE, seg_end[-1] - FFN_TILE), seg_end[-1:] // FFN_TILE])
    rows_max = TOP_K * B * T + (ROW_ALIGN - 1) * n_blocks * E + E * (CAP + FFN_TILE - 1)
    n_tiles = -(-rows_max // FFN_TILE)
    idx = jnp.arange(n_tiles, dtype=jnp.int32)
    n_valid = seg_end[-1] // FFN_TILE
    src = jnp.minimum(idx, n_valid - 1)
    tile_e = jnp.sum(seg_end[None, :] <= (src * FFN_TILE)[:, None], 1, dtype=jnp.int32)
    over = jnp.max(cnt, 0) > CAP
    return dict(start=start.reshape(-1), cnt=cnt.reshape(-1), over=over.astype(jnp.int32), zrow=zrow,
                tile_e=tile_e, src=src, valid=(idx < n_valid).astype(jnp.int32), n_tiles=n_tiles)


def _block_ranks(g, tri):
    routed = g > 0.0
    rank = jnp.dot(jnp.where(routed, 1.0, 0.0).astype(BF16), tri, preferred_element_type=F32)
    return jnp.where(routed, rank, -1.0)


def _group_copy(src, hbm, sem, row):
    return pltpu.make_async_copy(src, hbm.at[pl.ds(pl.multiple_of(row, ROW_ALIGN), CAP)], sem)


def _dispatch_kernel(start_ref, cnt_ref, over_ref, zrow_ref, h_ref, g_ref, tri_ref, xs_out, stage, xbuf, rank_s,
                     sem):
    E, tb = g_ref.shape[1], g_ref.shape[2]
    n_blocks = pl.num_programs(0) * pl.num_programs(1)
    blk = pl.program_id(0) * pl.num_programs(1) + pl.program_id(1)
    cur = blk % 2

    @pl.when(blk == 0)
    def _():
        zeros = stage.at[1, pl.ds(0, FFN_TILE)]

        def zero_copy(row):
            return pltpu.make_async_copy(zeros, xs_out.at[pl.ds(pl.multiple_of(row, FFN_TILE), FFN_TILE)], sem.at[1])

        stage[1, 0:FFN_TILE] = jnp.zeros((FFN_TILE, stage.shape[2]), stage.dtype)
        for i in range(E):
            zero_copy(zrow_ref[i]).start()
        for i in range(E):
            zero_copy(zrow_ref[i]).wait()
        for i in range(E, 2 * E):
            cp = zero_copy(zrow_ref[i])
            cp.start()
            cp.wait()
        first, last = zrow_ref[2 * E], xs_out.shape[0] // FFN_TILE
        lax.fori_loop(first, last, lambda i, c: (zero_copy(i * FFN_TILE).start(), c)[1], 0)
        lax.fori_loop(first, last, lambda i, c: (zero_copy(i * FFN_TILE).wait(), c)[1], 0)

    def group_copies(half, block):
        return [_group_copy(stage.at[half, pl.ds(e * CAP, CAP)], xs_out, sem.at[0], start_ref[e * n_blocks + block])
                for e in range(E)]

    rank = _block_ranks(g_ref[0], tri_ref[...])
    rank_s[...] = rank
    slot = lax.broadcasted_iota(jnp.int32, (CAP, tb), 0).astype(F32)
    h = h_ref[0]
    n_part = 4
    per = E // n_part
    for q in range(n_part):
        sel = jnp.concatenate([jnp.where(rank[e:e + 1] == slot, 1.0, 0.0).astype(BF16)
                               for e in range(q * per, (q + 1) * per)], axis=0)
        stage[cur, q * per * CAP:(q + 1) * per * CAP] = jnp.dot(sel, h, preferred_element_type=F32).astype(BF16)

    @pl.when(blk > 0)
    def _():
        for cp in group_copies(1 - cur, blk - 1):
            cp.wait()

    for cp in group_copies(cur, blk):
        cp.start()

    @pl.when(over_ref[blk] > 0)
    def _():
        def per_expert(e, carry):
            def per_chunk(c, carry):
                row = rank_s[pl.ds(e, 1), :] - (c * CAP).astype(F32)
                sel = jnp.where(row == slot, 1.0, 0.0).astype(BF16)
                xbuf[...] = jnp.dot(sel, h, preferred_element_type=F32).astype(BF16)
                cp = _group_copy(xbuf, xs_out, sem.at[1], start_ref[e * n_blocks + blk] + c * CAP)
                cp.start()
                cp.wait()
                return carry
            n_chunks = (cnt_ref[e * n_blocks + blk] + CAP - 1) // CAP
            return lax.fori_loop(1, n_chunks, per_chunk, carry)
        lax.fori_loop(0, E, per_expert, 0)

    @pl.when(blk == n_blocks - 1)
    def _():
        for cp in group_copies(cur, blk):
            cp.wait()


def _dispatch_call(h2, gates_t, tri, rt):
    B, T, D = h2.shape
    E = gates_t.shape[1]
    tb = tri.shape[0]
    rows = rt["n_tiles"] * FFN_TILE
    assert E * CAP >= FFN_TILE
    grid_spec = pltpu.PrefetchScalarGridSpec(
        num_scalar_prefetch=4,
        grid=(B, T // tb),
        in_specs=[pl.BlockSpec((1, tb, D), lambda b, t, *_: (b, t, 0)),
                  pl.BlockSpec((1, E, tb), lambda b, t, *_: (b, 0, t)),
                  pl.BlockSpec((tb, tb), lambda b, t, *_: (0, 0))],
        out_specs=pl.BlockSpec(memory_space=pl.ANY),
        scratch_shapes=[pltpu.VMEM((2, E * CAP, D), BF16), pltpu.VMEM((CAP, D), BF16), pltpu.VMEM((E, tb), F32),
                        pltpu.SemaphoreType.DMA((2,))],
    )
    return pl.pallas_call(
        _dispatch_kernel, grid_spec=grid_spec,
        out_shape=jax.ShapeDtypeStruct((rows, D), BF16),
        compiler_params=_params(("arbitrary", "arbitrary")),
        name="dispatch",
    )(rt["start"], rt["cnt"], rt["over"], rt["zrow"], h2, gates_t, tri)


def _ffn_kernel(te_ref, src_ref, valid_ref, x_ref, wgu_ref, bgu_ref, wd_ref, bdn_ref, o_ref):
    i = pl.program_id(0)

    @pl.when(valid_ref[i] > 0)
    def _():
        h = jnp.dot(x_ref[...], wgu_ref[0], preferred_element_type=F32) + bgu_ref[0]
        h_glu = jnp.minimum(h[:, :D_FF], SWIGLU_LIMIT)
        h_lin = jnp.clip(h[:, D_FF:], -SWIGLU_LIMIT, SWIGLU_LIMIT)
        y = (h_lin + 1.0) * (h_glu * _sigmoid(SWIGLU_ALPHA * h_glu))
        o_ref[...] = (_mm(y, wd_ref[0]) + bdn_ref[0]).astype(o_ref.dtype)

    @pl.when(valid_ref[i] == 0)
    def _():
        o_ref[...] = jnp.zeros_like(o_ref)


def _ffn_call(xs, rt, w_gu, b_gu, w_down, b_down):
    rows, D = xs.shape
    E = w_gu.shape[0]
    grid_spec = pltpu.PrefetchScalarGridSpec(
        num_scalar_prefetch=3,
        grid=(rt["n_tiles"],),
        in_specs=[pl.BlockSpec((FFN_TILE, D), lambda i, te, src, valid: (src[i], 0)),
                  pl.BlockSpec((1, D, 2 * D_FF), lambda i, te, src, valid: (te[i], 0, 0)),
                  pl.BlockSpec((1, 1, 2 * D_FF), lambda i, te, src, valid: (te[i], 0, 0)),
                  pl.BlockSpec((1, D_FF, D), lambda i, te, src, valid: (te[i], 0, 0)),
                  pl.BlockSpec((1, 1, D), lambda i, te, src, valid: (te[i], 0, 0))],
        out_specs=pl.BlockSpec((FFN_TILE, D), lambda i, te, src, valid: (i, 0)),
    )
    return pl.pallas_call(
        _ffn_kernel, grid_spec=grid_spec,
        out_shape=jax.ShapeDtypeStruct((rows, D), BF16),
        compiler_params=_params(("arbitrary",)),
        name="ffn",
    )(rt["tile_e"], rt["src"], rt["valid"], xs, w_gu, b_gu.reshape(E, 1, -1), w_down,
      b_down.reshape(E, 1, -1))


def _combine_kernel(start_ref, cnt_ref, over_ref, g_ref, x1_ref, mod_ref, tri_ref, lg_ref, lb_ref, ys_ref, o_ref,
                    ybuf, xbuf, rank_s, acc_s, sem):
    E, tb = g_ref.shape[1], g_ref.shape[2]
    n_blocks = pl.num_programs(0) * pl.num_programs(1)
    blk = pl.program_id(0) * pl.num_programs(1) + pl.program_id(1)
    cur = blk % 2

    def fetch(half, e, block):
        row = start_ref[e * n_blocks + block]
        return pltpu.make_async_copy(ys_ref.at[pl.ds(pl.multiple_of(row, ROW_ALIGN), CAP)],
                                     ybuf.at[half, pl.ds(e * CAP, CAP)], sem.at[half])

    @pl.when(blk == 0)
    def _():
        for e in range(E):
            fetch(0, e, 0).start()

    @pl.when(blk + 1 < n_blocks)
    def _():
        for e in range(E):
            fetch(1 - cur, e, blk + 1).start()

    g = g_ref[0]
    rank = _block_ranks(g, tri_ref[...])
    slot = lax.broadcasted_iota(jnp.int32, (CAP, tb), 0).astype(F32)
    for e in range(E):
        fetch(cur, e, blk).wait()
    n_part = 4
    per = E // n_part
    acc = None
    for q in range(n_part):
        w = jnp.concatenate([jnp.where(rank[e:e + 1] == slot, g[e:e + 1], 0.0).astype(BF16)
                             for e in range(q * per, (q + 1) * per)], axis=0)
        part = _mm_tn(w, ybuf[cur, q * per * CAP:(q + 1) * per * CAP])
        acc = part if acc is None else acc + part
    acc_s[...] = acc

    @pl.when(over_ref[blk] > 0)
    def _():
        rank_s[...] = rank

        def per_expert(e, carry):
            def per_chunk(c, carry):
                row0 = start_ref[e * n_blocks + blk] + c * CAP
                cp = pltpu.make_async_copy(ys_ref.at[pl.ds(pl.multiple_of(row0, ROW_ALIGN), CAP)], xbuf, sem.at[2])
                cp.start()
                row = rank_s[pl.ds(e, 1), :] - (c * CAP).astype(F32)
                w = jnp.where(row == slot, g_ref[0, pl.ds(e, 1), :], 0.0).astype(BF16)
                cp.wait()
                acc_s[...] += _mm_tn(w, xbuf[...])
                return carry
            n_chunks = (cnt_ref[e * n_blocks + blk] + CAP - 1) // CAP
            return lax.fori_loop(1, n_chunks, per_chunk, carry)
        lax.fori_loop(0, E, per_expert, 0)

    m = mod_ref[0]
    o_ref[0] = _ln(DEEPNORM_ALPHA * x1_ref[0] + (1.0 + m[5:6]) * acc_s[...], lg_ref[...], lb_ref[...], LN_EPS)


def _combine_call(ys, gates_t, x1, mod, tri, rt, ln2_g, ln2_b):
    B, T, D = x1.shape
    E = gates_t.shape[1]
    tb = tri.shape[0]
    const = lambda b, t, *_: (0, 0)
    grid_spec = pltpu.PrefetchScalarGridSpec(
        num_scalar_prefetch=3,
        grid=(B, T // tb),
        in_specs=[pl.BlockSpec((1, E, tb), lambda b, t, *_: (b, 0, t)),
                  pl.BlockSpec((1, tb, D), lambda b, t, *_: (b, t, 0)),
                  pl.BlockSpec((1, 6, D), lambda b, t, *_: (b, 0, 0)),
                  pl.BlockSpec((tb, tb), const),
                  pl.BlockSpec((1, D), const), pl.BlockSpec((1, D), const),
                  pl.BlockSpec(memory_space=pl.ANY)],
        out_specs=pl.BlockSpec((1, tb, D), lambda b, t, *_: (b, t, 0)),
        scratch_shapes=[pltpu.VMEM((2, E * CAP, D), BF16), pltpu.VMEM((CAP, D), BF16), pltpu.VMEM((E, tb), F32),
                        pltpu.VMEM((tb, D), F32), pltpu.SemaphoreType.DMA((3,))],
    )
    return pl.pallas_call(
        _combine_kernel, grid_spec=grid_spec,
        out_shape=jax.ShapeDtypeStruct((B, T, D), F32),
        compiler_params=_params(("arbitrary", "arbitrary")),
        name="combine",
    )(rt["start"], rt["cnt"], rt["over"], gates_t, x1, mod, tri, _row(ln2_g), _row(ln2_b), ys)


def _moe_call(h2, gates_t, x1, mod, w_gu, b_gu, w_down, b_down, ln2_g, ln2_b):
    tb = min(TOKEN_BLOCK, h2.shape[1])
    pos = lax.broadcasted_iota(jnp.int32, (tb, tb), 0)
    tri = (pos < pos.T).astype(BF16)
    rt = _routing(gates_t, tb)
    xs = _dispatch_call(h2, gates_t, tri, rt)
    ys = _ffn_call(xs, rt, w_gu, b_gu, w_down, b_down)
    return _combine_call(ys, gates_t, x1, mod, tri, rt, ln2_g, ln2_b)


def _block_diag_cat(w):
    z = jnp.zeros_like(w[0])
    return jnp.concatenate([jnp.concatenate([w[0], z], 1), jnp.concatenate([z, w[1]], 1)], 0)


def _encode(x, mod, W):
    p_r, p_c = _inproj_call(x, mod, W["in_g"], W["in_b"], W["w_in_r"], W["w_in_c"])
    r, v, kk, lw, kd, bb, g, bonus = _pre_call(p_r, W["mu_shift"], W["w0c"], W["w2c"], W["a0c"], W["a2c"],
                                               W["g_up"], W["k_k"], W["k_a"], W["r_k"], W["bd"])
    wkv_f, wkv_b = _scan_call(r, v, kk, lw, kd, bb)
    y_b = _conv_call(p_c, W["dw_w"], W["dw_b"], W["cln_g"], W["cln_b"])
    x1, h2, gates = _post_call(x, mod, W["in_g"], W["in_b"], wkv_f, wkv_b, bonus, g, y_b, W["gn_g"], W["gn_b"],
                               W["bd"], W["w_out"], W["ln1_g"], W["ln1_b"], W["w_router_p"], W["b_router_p"])
    return _moe_call(h2, gates, x1, mod, W["w_gu"], W["b_gu"], W["w_down"], W["b_down"], W["ln2_g"], W["ln2_b"])


def kernel(x_prompt, x_sample, c_prompt, c_sample, in_g, in_b, w_mod, b_mod, w_in, mu_shift, w0, w2, a0, a2, g_up, k_k, k_a, r_k, gn_g, gn_b, dw_w, dw_b, cln_g, cln_b, w_out, ln1_g, ln1_b, w_router, b_router, w_gu, b_gu, w_down, b_down, ln2_g, ln2_b):
    l = 0
    head = lax.broadcasted_iota(jnp.int32, (D_RWKV, D_RWKV), 0) // HEAD_DIM
    W = dict(
        in_g=in_g, in_b=in_b,
        w_in_r=w_in[l][:, :N_SHIFT_COLS].astype(BF16), w_in_c=w_in[l][:, N_SHIFT_COLS:].astype(BF16),
        mu_shift=mu_shift[l],
        w0c=w0[l].reshape(1, -1), w2c=_block_diag_cat(w2[l]).astype(BF16),
        a0c=a0[l].reshape(1, -1), a2c=_block_diag_cat(a2[l]).astype(BF16),
        g_up=g_up[l].astype(BF16), k_k=_row(k_k[l]), k_a=_row(k_a[l]), r_k=_row(r_k[l]),
        bd=(head == head.T).astype(BF16),
        gn_g=gn_g[l], gn_b=gn_b[l], dw_w=dw_w[l], dw_b=dw_b[l], cln_g=cln_g[l], cln_b=cln_b[l],
        w_out=w_out[l].astype(BF16), ln1_g=ln1_g[l], ln1_b=ln1_b[l],
        w_router_p=jnp.pad(w_router[l], ((0, 0), (0, LANES - N_EXPERTS))),
        b_router_p=jnp.pad(b_router[l], (0, LANES - N_EXPERTS), constant_values=-1e30).reshape(1, -1),
        w_gu=w_gu[l].astype(BF16), b_gu=b_gu[l], w_down=w_down[l].astype(BF16), b_down=b_down[l],
        ln2_g=ln2_g[l], ln2_b=ln2_b[l],
    )
    nb = x_prompt.shape[0]
    mod = _mod_call(jnp.concatenate([c_prompt, c_sample], 0), w_mod[l], b_mod[l])
    return (_encode(x_prompt, mod[:nb], W), _encode(x_sample, mod[nb:], W))
```

```python
import math

import jax
import jax.numpy as jnp
from jax import lax
from jax.experimental import pallas as pl
from jax.experimental.pallas import tpu as pltpu

F32 = jnp.float32
BF16 = jnp.bfloat16

D_MODEL = 1024
D_RWKV = 512
HEAD_DIM = 64
N_HEADS = 8
D_CONV = 512
DECAY_RANK = 64
AAA_RANK = 64
GATE_RANK = 128
CONV_WIDTH = 31
N_EXPERTS = 32
TOP_K = 4
D_FF = 1024
SWIGLU_LIMIT = 7.0
SWIGLU_ALPHA = 1.702
LN_EPS = 1e-5
GN_EPS = 64e-5
DEPTH = 1
DEEPNORM_ALPHA = (2.0 * DEPTH) ** 0.25
EXP_NEG_HALF = math.exp(-0.5)
N_SHIFT_COLS = 3 * D_RWKV + 2 * DECAY_RANK + 2 * AAA_RANK + GATE_RANK

LANES = 128
SUBLANES = 8
HALO = 16
CHUNK = 64
SCAN_BLOCK = 8 * CHUNK
TOKEN_BLOCK = 512
CAP = 128
ROW_ALIGN = 16
FFN_TILE = 1024
GROUP = 4
GROUP_W = GROUP * HEAD_DIM
VMEM_LIMIT = 56 * 1024 * 1024


def _params(sem, flags=None):
    return pltpu.CompilerParams(dimension_semantics=sem, vmem_limit_bytes=VMEM_LIMIT, flags=flags)


def _sigmoid(x):
    return 1.0 / (1.0 + jnp.exp(-x))


def _mm(a, b):
    return jnp.dot(a.astype(BF16), b.astype(BF16), preferred_element_type=F32)


def _mm_nt(a, b):
    return lax.dot_general(a.astype(BF16), b.astype(BF16), (((1,), (1,)), ((), ())),
                           preferred_element_type=F32)


def _mm_tn(a, b):
    return lax.dot_general(a.astype(BF16), b.astype(BF16), (((0,), (0,)), ((), ())),
                           preferred_element_type=F32)


def _split(a, n):
    parts = []
    for _ in range(n):
        h = a.astype(BF16)
        parts.append(h)
        a = a - h.astype(F32)
    return parts


def _mm_exact_rhs(a, b_bf16, n):
    acc = None
    for h in _split(a, n):
        t = jnp.dot(h, b_bf16, preferred_element_type=F32)
        acc = t if acc is None else acc + t
    return acc


def _mm_exact_lhs(a_bf16, b, n):
    acc = None
    for h in _split(b, n):
        t = jnp.dot(a_bf16, h, preferred_element_type=F32)
        acc = t if acc is None else acc + t
    return acc


def _mm3(a, b):
    ah, al = _split(a, 2)
    bh, bl = _split(b, 2)
    return (jnp.dot(ah, bh, preferred_element_type=F32) + jnp.dot(al, bh, preferred_element_type=F32)
            + jnp.dot(ah, bl, preferred_element_type=F32))


def _ln(x, g, b, eps):
    mu = jnp.mean(x, -1, keepdims=True)
    xc = x - mu
    var = jnp.mean(xc * xc, -1, keepdims=True)
    return xc * lax.rsqrt(var + eps) * g + b


def _row(a):
    return a.reshape(1, -1)


def _mod_kernel(c_ref, w_ref, b_ref, o_ref):
    c = c_ref[...]
    o_ref[...] = _mm3(c * _sigmoid(c), w_ref[...]) + b_ref[...]


def _mod_call(c, w_mod, b_mod):
    n, d = c.shape
    npad = -(-n // 8) * 8
    cp = jnp.pad(c, ((0, npad - n), (0, 0)))
    tn = 1536
    out = pl.pallas_call(
        _mod_kernel,
        grid=(6 * d // tn,),
        in_specs=[pl.BlockSpec((npad, d), lambda j: (0, 0)),
                  pl.BlockSpec((d, tn), lambda j: (0, j)),
                  pl.BlockSpec((1, tn), lambda j: (0, j))],
        out_specs=pl.BlockSpec((npad, tn), lambda j: (0, j)),
        out_shape=jax.ShapeDtypeStruct((npad, 6 * d), F32),
        compiler_params=_params(("arbitrary",)),
        name="mod",
    )(cp, w_mod, _row(b_mod))
    return out[:n].reshape(n, 6, d)


def _inproj_kernel(x_ref, mod_ref, g_ref, b_ref, wr_ref, wc_ref, pr_ref, pc_ref):
    x0 = _ln(x_ref[0], g_ref[...], b_ref[...], LN_EPS)
    m = mod_ref[0]
    h = (x0 * (1.0 + m[1:2]) + m[0:1]).astype(BF16)
    pr_ref[0] = jnp.dot(h, wr_ref[...], preferred_element_type=F32).astype(pr_ref.dtype)
    pc_ref[0] = jnp.dot(h, wc_ref[...], preferred_element_type=F32).astype(pc_ref.dtype)


def _inproj_call(x, mod, in_g, in_b, w_r, w_c):
    B, T, D = x.shape
    tm = min(512, T)
    const = lambda b, t: (0, 0)
    return pl.pallas_call(
        _inproj_kernel,
        grid=(B, T // tm),
        in_specs=[pl.BlockSpec((1, tm, D), lambda b, t: (b, t, 0)),
                  pl.BlockSpec((1, 6, D), lambda b, t: (b, 0, 0)),
                  pl.BlockSpec((1, D), const), pl.BlockSpec((1, D), const),
                  pl.BlockSpec(w_r.shape, const), pl.BlockSpec(w_c.shape, const)],
        out_specs=[pl.BlockSpec((1, tm, N_SHIFT_COLS), lambda b, t: (b, t, 0)),
                   pl.BlockSpec((1, tm, 2 * D_CONV), lambda b, t: (b, t, 0))],
        out_shape=[jax.ShapeDtypeStruct((B, T, N_SHIFT_COLS), BF16),
                   jax.ShapeDtypeStruct((B, T, 2 * D_CONV), BF16)],
        compiler_params=_params(("parallel", "parallel")),
        name="inproj",
    )(x, mod, _row(in_g), _row(in_b), w_r, w_c)


def _pre_kernel(p_ref, pp_ref, pn_ref, mu_ref, w0_ref, w2_ref, a0_ref, a2_ref, gup_ref, kkw_ref, ka_ref,
                rk_ref, bd_ref, r_o, v_o, kk_o, lw_o, kd_o, bb_o, g_o, bonus_o):
    t = pl.program_id(1)
    nt = pl.num_programs(1)
    p = p_ref[0].astype(F32)
    tm = p.shape[0]
    prev_row = jnp.where(t > 0, pp_ref[0].astype(F32)[HALO - 1:HALO], 0.0)
    next_row = jnp.where(t < nt - 1, pn_ref[0].astype(F32)[0:1], 0.0)
    row = lax.broadcasted_iota(jnp.int32, p.shape, 0)
    prev = jnp.where(row == 0, prev_row, pltpu.roll(p, 1, 0))
    nxt = jnp.where(row == tm - 1, next_row, pltpu.roll(p, tm - 1, 0))
    mu = mu_ref[...]
    ps = p * (1.0 - mu[0:1] - mu[1:2]) + mu[0:1] * prev + mu[1:2] * nxt

    r = ps[:, 0:D_RWKV]
    k = ps[:, D_RWKV:2 * D_RWKV]
    v = ps[:, 2 * D_RWKV:3 * D_RWKV]
    o = 3 * D_RWKV
    wl = ps[:, o:o + 2 * DECAY_RANK]
    al = ps[:, o + 2 * DECAY_RANK:o + 2 * DECAY_RANK + 2 * AAA_RANK]
    gl = ps[:, o + 2 * DECAY_RANK + 2 * AAA_RANK:]

    lw = -EXP_NEG_HALF * _sigmoid(w0_ref[...] + _mm(jnp.tanh(wl), w2_ref[...]))
    a = _sigmoid(a0_ref[...] + _mm(al, a2_ref[...]))
    g = _mm(_sigmoid(gl), gup_ref[...])

    bd = bd_ref[...]
    kk = k * kkw_ref[...]
    ss = _mm_exact_rhs(kk * kk, bd, 1)
    kk = kk * lax.rsqrt(jnp.maximum(ss, 1e-24))
    ka = ka_ref[...]
    a0d, a1d = a[:, :D_RWKV], a[:, D_RWKV:]
    k0 = k * (1.0 + (a0d - 1.0) * ka)
    k1 = k * (1.0 + (a1d - 1.0) * ka)
    bonus = _mm_exact_rhs(r * (k0 + k1) * rk_ref[...], bd, 1) * v

    r_o[0] = r.astype(r_o.dtype)
    v_o[0] = v.astype(v_o.dtype)
    kk_o[0] = kk.astype(kk_o.dtype)
    lw_o[0] = lw
    kd_o[0, :, :D_RWKV] = k0.astype(kd_o.dtype)
    kd_o[0, :, D_RWKV:] = k1.astype(kd_o.dtype)
    bb_o[0, :, :D_RWKV] = (kk * a0d).astype(bb_o.dtype)
    bb_o[0, :, D_RWKV:] = (kk * a1d).astype(bb_o.dtype)
    g_o[0] = g.astype(g_o.dtype)
    bonus_o[0] = bonus.astype(bonus_o.dtype)


def _pre_call(p_r, mu_shift, w0c, w2c, a0c, a2c, g_up, k_k, k_a, r_k, bd):
    B, T, C = p_r.shape
    tm = min(256, T)
    nh = tm // HALO
    const = lambda b, t: (0, 0)
    tile = lambda w: pl.BlockSpec((1, tm, w), lambda b, t: (b, t, 0))
    outs = [D_RWKV, D_RWKV, D_RWKV, 2 * D_RWKV, 2 * D_RWKV, 2 * D_RWKV, D_RWKV, D_RWKV]
    return pl.pallas_call(
        _pre_kernel,
        grid=(B, T // tm),
        in_specs=[tile(C),
                  pl.BlockSpec((1, HALO, C), lambda b, t: (b, jnp.maximum(t * nh - 1, 0), 0)),
                  pl.BlockSpec((1, HALO, C), lambda b, t: (b, jnp.minimum((t + 1) * nh, T // HALO - 1), 0)),
                  pl.BlockSpec(mu_shift.shape, const), pl.BlockSpec(w0c.shape, const),
                  pl.BlockSpec(w2c.shape, const), pl.BlockSpec(a0c.shape, const),
                  pl.BlockSpec(a2c.shape, const), pl.BlockSpec(g_up.shape, const),
                  pl.BlockSpec(k_k.shape, const), pl.BlockSpec(k_a.shape, const),
                  pl.BlockSpec(r_k.shape, const), pl.BlockSpec(bd.shape, const)],
        out_specs=[tile(w) for w in outs],
        out_shape=[jax.ShapeDtypeStruct((B, T, w), F32 if i == 3 else BF16) for i, w in enumerate(outs)],
        compiler_params=_params(("parallel", "parallel")),
        name="pre",
    )(p_r, p_r, p_r, mu_shift, w0c, w2c, a0c, a2c, g_up, k_k, k_a, r_k, bd)


def _chains_chunk(chains):
    assert CHUNK == HEAD_DIM
    L = chains[0]["rt"].shape[0]
    R = GROUP * L
    rowh = lax.broadcasted_iota(jnp.int32, (R, GROUP_W), 0) // L
    colh = lax.broadcasted_iota(jnp.int32, (R, GROUP_W), 1) // HEAD_DIM
    own = rowh == colh

    def bd(x):
        return jnp.where(own, jnp.concatenate([x.astype(BF16)] * GROUP, axis=0), jnp.zeros((), BF16))

    def dot(a, b):
        return jnp.dot(a.astype(BF16), b, preferred_element_type=F32)

    t_row = lax.broadcasted_iota(jnp.int32, (L, GROUP_W), 0)
    i_col = lax.broadcasted_iota(jnp.int32, (L, GROUP_W), 1) % L
    eye = jnp.where(t_row == i_col, 1.0, 0.0)
    masks = {False: (i_col < t_row, i_col <= t_row), True: (i_col > t_row, i_col >= t_row)}

    zr = [jnp.concatenate([c["zt"], c["rt"]], axis=0).astype(BF16) for c in chains]
    a_b = [_mm_nt(x, bd(c["bt"])) for x, c in zip(zr, chains)]
    a_k = [_mm_nt(x, bd(c["kt"])) for x, c in zip(zr, chains)]
    n_zb = [jnp.where(masks[c["reverse"]][0], a[:L], 0.0) for c, a in zip(chains, a_b)]
    a_rb = [jnp.where(masks[c["reverse"]][1], a[L:], 0.0) for c, a in zip(chains, a_b)]
    a_zr = [jnp.concatenate([jnp.where(masks[c["reverse"]][0], a[:L], 0.0),
                             jnp.where(masks[c["reverse"]][1], a[L:], 0.0)], axis=0) for c, a in zip(chains, a_k)]

    tinv = [eye + n for n in n_zb]
    pw = [dot(n, bd(n)) for n in n_zb]
    levels = L.bit_length() - 2
    for j in range(1, levels + 1):
        if j < levels:
            both = [dot(jnp.concatenate([p, t], axis=0), bd(p)) for p, t in zip(pw, tinv)]
            pw = [b[:L] for b in both]
            tinv = [t + b[L:] for t, b in zip(tinv, both)]
        else:
            tinv = [t + dot(t, bd(p)) for p, t in zip(pw, tinv)]

    av = [dot(a, bd(c["v"])) for a, c in zip(a_zr, chains)]

    def rows(x):
        return jnp.concatenate([x[:, h * HEAD_DIM:(h + 1) * HEAD_DIM] for h in range(GROUP)], axis=0)

    def advance(idx, S):
        zs = [_mm_nt(zr[i], bd(s)) for i, s in zip(idx, S)]
        u = [dot(tinv[i], bd(z[:L] + av[i][:L])) for i, z in zip(idx, zs)]
        outs = [z[L:] + av[i][L:] + dot(a_rb[i], bd(x)) for i, z, x in zip(idx, zs, u)]
        nxt = [s * chains[i]["wtot"] + _mm_tn(rows(x), bd(chains[i]["bhat"]))
               + _mm_tn(rows(chains[i]["v"]), bd(chains[i]["khat"])) for i, s, x in zip(idx, S, u)]
        return outs, nxt

    return advance


def _prep_chains(r, v, kk, lw, kd, bb, reverse):
    L = r.shape[0]
    r, kk, kd, bb = (x.astype(F32) for x in (r, kk, kd, bb))
    ti = lax.broadcasted_iota(jnp.int32, (L, L), 0)
    ii = lax.broadcasted_iota(jnp.int32, (L, L), 1)
    tri = jnp.where((ii >= ti) if reverse else (ii <= ti), 1.0, 0.0).astype(BF16)
    cum = _mm_exact_lhs(tri, lw, 3)
    tot = cum[0:1] if reverse else cum[L - 1:L]
    ec, eci, ecx, eto = jnp.exp(cum), jnp.exp(-cum), jnp.exp(cum - lw), jnp.exp(tot - cum)
    full = dict(rt=r * ec, zt=-(kk * ecx), kt=kd * eci, bt=bb * eci, khat=kd * eto, bhat=bb * eto, v=v,
                wtot=jnp.exp(tot))
    chains = []
    for g in range(D_RWKV // GROUP_W):
        c = {k: x[:, g * GROUP_W:(g + 1) * GROUP_W] for k, x in full.items()}
        c["reverse"] = reverse
        chains.append(c)
    return chains


def _scan_kernel(rf, vf, kkf, lwf, kdf, bbf, rb, vb, kkb, lwb, kdb, bbb, of, ob, s_ref):
    @pl.when(pl.program_id(1) == 0)
    def _():
        s_ref[...] = jnp.zeros_like(s_ref)

    n_sub = rf.shape[1] // CHUNK
    n_grp = D_RWKV // GROUP_W
    per = 2 * n_grp
    chains, spans = [], []
    for i in range(n_sub):
        lf = pl.ds(i * CHUNK, CHUNK)
        lb = pl.ds((n_sub - 1 - i) * CHUNK, CHUNK)
        spans.append((lf, lb))
        chains += _prep_chains(rf[0, lf], vf[0, lf], kkf[0, lf], lwf[0, lf], kdf[0, lf], bbf[0, lf], False)
        chains += _prep_chains(rb[0, lb], vb[0, lb], kkb[0, lb], lwb[0, lb], kdb[0, lb], bbb[0, lb], True)
    advance = _chains_chunk(chains)
    sf, sb = s_ref[0], s_ref[1]
    states = ([sf[:, g * GROUP_W:(g + 1) * GROUP_W] for g in range(n_grp)]
              + [sb[:, g * GROUP_W:(g + 1) * GROUP_W] for g in range(n_grp)])
    for i, (lf, lb) in enumerate(spans):
        outs, states = advance(list(range(i * per, (i + 1) * per)), states)
        of[0, lf] = jnp.concatenate(outs[:n_grp], axis=1)
        ob[0, lb] = jnp.concatenate(outs[n_grp:], axis=1)
    s_ref[0] = jnp.concatenate(states[:n_grp], axis=1)
    s_ref[1] = jnp.concatenate(states[n_grp:], axis=1)


def _scan_call(r, v, kk, lw, kd, bb):
    B, T, C = r.shape
    L = min(SCAN_BLOCK, T)
    nc = T // L
    fwd = lambda j: pl.BlockSpec((1, L, C), lambda b, c: (b, c, j))
    bwd = lambda j: pl.BlockSpec((1, L, C), lambda b, c: (b, nc - 1 - c, j))
    return pl.pallas_call(
        _scan_kernel,
        grid=(B, nc),
        in_specs=[fwd(0)] * 6 + [bwd(0)] * 3 + [bwd(1)] * 3,
        out_specs=[fwd(0), bwd(0)],
        out_shape=[jax.ShapeDtypeStruct((B, T, C), F32)] * 2,
        scratch_shapes=[pltpu.VMEM((2, HEAD_DIM, C), F32)],
        compiler_params=_params(("parallel", "arbitrary")),
        name="scan",
    )(r, v, kk, lw, kd, bb, r, v, kk, lw, kd, bb)


def _conv_kernel(p_ref, pp_ref, pn_ref, w_ref, b_ref, g_ref, be_ref, o_ref, h_ref, hs_ref):
    t = pl.program_id(1)
    nt = pl.num_programs(1)
    tm = p_ref.shape[1]

    def glu(x):
        x = x.astype(F32)
        return x[:, :D_CONV] * _sigmoid(x[:, D_CONV:])

    h_ref[0:HALO] = jnp.where(t > 0, glu(pp_ref[0]), 0.0)
    h_ref[HALO:HALO + tm] = glu(p_ref[0])
    h_ref[HALO + tm:] = jnp.where(t < nt - 1, glu(pn_ref[0]), 0.0)
    rows = hs_ref.shape[1]
    for s in range(1, SUBLANES):
        hs_ref[s - 1] = h_ref[s:s + rows]
    w = w_ref[...]
    acc = jnp.zeros((tm, D_CONV), F32) + b_ref[...]
    base = HALO - CONV_WIDTH // 2
    for j in range(CONV_WIDTH):
        off, s = (base + j) // SUBLANES * SUBLANES, (base + j) % SUBLANES
        tap = h_ref[off:off + tm] if s == 0 else hs_ref[s - 1, off:off + tm]
        acc = acc + w[j:j + 1] * tap
    y = _ln(acc, g_ref[...], be_ref[...], LN_EPS)
    o_ref[0] = (y * _sigmoid(y)).astype(o_ref.dtype)


def _conv_call(p_c, dw_w, dw_b, cln_g, cln_b):
    B, T, C = p_c.shape
    tm = min(256, T)
    nh = tm // HALO
    const = lambda b, t: (0, 0)
    return pl.pallas_call(
        _conv_kernel,
        grid=(B, T // tm),
        in_specs=[pl.BlockSpec((1, tm, C), lambda b, t: (b, t, 0)),
                  pl.BlockSpec((1, HALO, C), lambda b, t: (b, jnp.maximum(t * nh - 1, 0), 0)),
                  pl.BlockSpec((1, HALO, C), lambda b, t: (b, jnp.minimum((t + 1) * nh, T // HALO - 1), 0)),
                  pl.BlockSpec(dw_w.shape, const), pl.BlockSpec((1, D_CONV), const),
                  pl.BlockSpec((1, D_CONV), const), pl.BlockSpec((1, D_CONV), const)],
        out_specs=pl.BlockSpec((1, tm, D_CONV), lambda b, t: (b, t, 0)),
        out_shape=jax.ShapeDtypeStruct((B, T, D_CONV), BF16),
        scratch_shapes=[pltpu.VMEM((tm + 2 * HALO, D_CONV), F32),
                        pltpu.VMEM((SUBLANES - 1, tm + 2 * HALO - SUBLANES, D_CONV), F32)],
        compiler_params=_params(("parallel", "parallel")),
        name="conv",
    )(p_c, p_c, p_c, dw_w, _row(dw_b), _row(cln_g), _row(cln_b))


def _post_kernel(x_ref, mod_ref, ing_ref, inb_ref, wf_ref, wb_ref, bonus_ref, g_ref, yb_ref, gng_ref, gnb_ref,
                 bd_ref, wo_ref, l1g_ref, l1b_ref, wr_ref, br_ref, x1_o, h2_o, gate_o):
    m = mod_ref[0]
    x0 = _ln(x_ref[0], ing_ref[...], inb_ref[...], LN_EPS)
    bd = bd_ref[...]
    wkv = wf_ref[0] + wb_ref[0]
    mu = _mm_exact_rhs(wkv, bd, 2) * (1.0 / HEAD_DIM)
    wc = wkv - mu
    var = _mm_exact_rhs(wc * wc, bd, 1) * (1.0 / HEAD_DIM)
    o = wc * lax.rsqrt(var + GN_EPS) * gng_ref[...] + gnb_ref[...]
    ya = (o + bonus_ref[0].astype(F32)) * g_ref[0].astype(F32)
    wo = wo_ref[...]
    mix = _mm(ya, wo[:D_RWKV]) + jnp.dot(yb_ref[0], wo[D_RWKV:], preferred_element_type=F32)
    x1 = _ln(DEEPNORM_ALPHA * x0 + (1.0 + m[2:3]) * mix, l1g_ref[...], l1b_ref[...], LN_EPS)
    x1_o[0] = x1
    h2 = x1 * (1.0 + m[4:5]) + m[3:4]
    h2_o[0] = h2.astype(h2_o.dtype)

    logits = _mm3(h2, wr_ref[...]) + br_ref[...]
    lane = lax.broadcasted_iota(jnp.int32, logits.shape, 1).astype(F32)
    work = logits
    sel = jnp.zeros(logits.shape, F32)
    top = None
    for _ in range(TOP_K):
        mx = jnp.max(work, -1, keepdims=True)
        first = jnp.min(jnp.where(work == mx, lane, float(LANES)), -1, keepdims=True)
        pick = lane == first
        sel = jnp.where(pick, 1.0, sel)
        work = jnp.where(pick, -jnp.inf, work)
        top = mx if top is None else top
    e = jnp.where(sel > 0.0, jnp.exp(logits - top), 0.0)
    gates = e / jnp.sum(e, -1, keepdims=True)
    gate_o[0] = gates.T[:N_EXPERTS]


def _post_call(x, mod, in_g, in_b, wkv_f, wkv_b, bonus, g, y_b, gn_g, gn_b, bd, w_out, ln1_g, ln1_b,
               w_router_p, b_router_p):
    B, T, D = x.shape
    tm = min(256, T)
    const = lambda b, t: (0, 0)
    tile = lambda w: pl.BlockSpec((1, tm, w), lambda b, t: (b, t, 0))
    vec = lambda w: pl.BlockSpec((1, w), const)
    return pl.pallas_call(
        _post_kernel,
        grid=(B, T // tm),
        in_specs=[tile(D), pl.BlockSpec((1, 6, D), lambda b, t: (b, 0, 0)), vec(D), vec(D),
                  tile(D_RWKV), tile(D_RWKV), tile(D_RWKV), tile(D_RWKV), tile(D_CONV),
                  vec(D_RWKV), vec(D_RWKV), pl.BlockSpec(bd.shape, const),
                  pl.BlockSpec(w_out.shape, const), vec(D), vec(D),
                  pl.BlockSpec(w_router_p.shape, const), vec(LANES)],
        out_specs=[tile(D), tile(D), pl.BlockSpec((1, N_EXPERTS, tm), lambda b, t: (b, 0, t))],
        out_shape=[jax.ShapeDtypeStruct((B, T, D), F32), jax.ShapeDtypeStruct((B, T, D), BF16),
                   jax.ShapeDtypeStruct((B, N_EXPERTS, T), F32)],
        compiler_params=_params(("parallel", "parallel")),
        name="post",
    )(x, mod, _row(in_g), _row(in_b), wkv_f, wkv_b, bonus, g, y_b, _row(gn_g), _row(gn_b), bd, w_out,
      _row(ln1_g), _row(ln1_b), w_router_p, b_router_p)


def _routing(gates_t, tb):
    B, E, T = gates_t.shape
    nb = T // tb
    n_blocks = B * nb
    cnt = jnp.sum((gates_t > 0.0).reshape(B, E, nb, tb), -1, dtype=jnp.int32)
    cnt = cnt.transpose(1, 0, 2).reshape(E, n_blocks)
    cnt_al = (cnt + ROW_ALIGN - 1) // ROW_ALIGN * ROW_ALIGN
    n_e = jnp.sum(cnt_al, 1)
    seg = (n_e + CAP + FFN_TILE - 1) // FFN_TILE * FFN_TILE
    seg_end = jnp.cumsum(seg)
    seg_start = seg_end - seg
    start = seg_start[:, None] + jnp.cumsum(cnt_al, 1) - cnt_al
    z0 = seg_start + n_e // FFN_TILE * FFN_TILE
    zrow = jnp.concatenate([z0, jnp.minimum(z0 + FFN_TILE, seg_end[-1] - FFN_TILE), seg_end[-1:] // FFN_TILE])
    rows_max = TOP_K * B * T + (ROW_ALIGN - 1) * n_blocks * E + E * (CAP + FFN_TILE - 1)
    n_tiles = -(-rows_max // FFN_TILE)
    idx = jnp.arange(n_tiles, dtype=jnp.int32)
    n_valid = seg_end[-1] // FFN_TILE
    src = jnp.minimum(idx, n_valid - 1)
    tile_e = jnp.sum(seg_end[None, :] <= (src * FFN_TILE)[:, None], 1, dtype=jnp.int32)
    over = jnp.max(cnt, 0) > CAP
    return dict(start=start.reshape(-1), cnt=cnt.reshape(-1), over=over.astype(jnp.int32), zrow=zrow,
                tile_e=tile_e, src=src, valid=(idx < n_valid).astype(jnp.int32), n_tiles=n_tiles)


def _block_ranks(g, tri):
    routed = g > 0.0
    rank = jnp.dot(jnp.where(routed, 1.0, 0.0).astype(BF16), tri, preferred_element_type=F32)
    return jnp.where(routed, rank, -1.0)


def _group_copy(src, hbm, sem, row):
    return pltpu.make_async_copy(src, hbm.at[pl.ds(pl.multiple_of(row, ROW_ALIGN), CAP)], sem)


def _dispatch_kernel(start_ref, cnt_ref, over_ref, zrow_ref, h_ref, g_ref, tri_ref, xs_out, stage, xbuf, rank_s,
                     sem):
    E, tb = g_ref.shape[1], g_ref.shape[2]
    n_blocks = pl.num_programs(0) * pl.num_programs(1)
    blk = pl.program_id(0) * pl.num_programs(1) + pl.program_id(1)
    cur = blk % 2

    @pl.when(blk == 0)
    def _():
        zeros = stage.at[1, pl.ds(0, FFN_TILE)]

        def zero_copy(row):
            return pltpu.make_async_copy(zeros, xs_out.at[pl.ds(pl.multiple_of(row, FFN_TILE), FFN_TILE)], sem.at[1])

        stage[1, 0:FFN_TILE] = jnp.zeros((FFN_TILE, stage.shape[2]), stage.dtype)
        for i in range(E):
            zero_copy(zrow_ref[i]).start()
        for i in range(E):
            zero_copy(zrow_ref[i]).wait()
        for i in range(E, 2 * E):
            cp = zero_copy(zrow_ref[i])
            cp.start()
            cp.wait()
        first, last = zrow_ref[2 * E], xs_out.shape[0] // FFN_TILE
        lax.fori_loop(first, last, lambda i, c: (zero_copy(i * FFN_TILE).start(), c)[1], 0)
        lax.fori_loop(first, last, lambda i, c: (zero_copy(i * FFN_TILE).wait(), c)[1], 0)

    def group_copies(half, block):
        return [_group_copy(stage.at[half, pl.ds(e * CAP, CAP)], xs_out, sem.at[0], start_ref[e * n_blocks + block])
                for e in range(E)]

    rank = _block_ranks(g_ref[0], tri_ref[...])
    rank_s[...] = rank
    slot = lax.broadcasted_iota(jnp.int32, (CAP, tb), 0).astype(F32)
    h = h_ref[0]
    n_part = 4
    per = E // n_part
    for q in range(n_part):
        sel = jnp.concatenate([jnp.where(rank[e:e + 1] == slot, 1.0, 0.0).astype(BF16)
                               for e in range(q * per, (q + 1) * per)], axis=0)
        stage[cur, q * per * CAP:(q + 1) * per * CAP] = jnp.dot(sel, h, preferred_element_type=F32).astype(BF16)

    @pl.when(blk > 0)
    def _():
        for cp in group_copies(1 - cur, blk - 1):
            cp.wait()

    for cp in group_copies(cur, blk):
        cp.start()

    @pl.when(over_ref[blk] > 0)
    def _():
        def per_expert(e, carry):
            def per_chunk(c, carry):
                row = rank_s[pl.ds(e, 1), :] - (c * CAP).astype(F32)
                sel = jnp.where(row == slot, 1.0, 0.0).astype(BF16)
                xbuf[...] = jnp.dot(sel, h, preferred_element_type=F32).astype(BF16)
                cp = _group_copy(xbuf, xs_out, sem.at[1], start_ref[e * n_blocks + blk] + c * CAP)
                cp.start()
                cp.wait()
                return carry
            n_chunks = (cnt_ref[e * n_blocks + blk] + CAP - 1) // CAP
            return lax.fori_loop(1, n_chunks, per_chunk, carry)
        lax.fori_loop(0, E, per_expert, 0)

    @pl.when(blk == n_blocks - 1)
    def _():
        for cp in group_copies(cur, blk):
            cp.wait()


def _dispatch_call(h2, gates_t, tri, rt):
    B, T, D = h2.shape
    E = gates_t.shape[1]
    tb = tri.shape[0]
    rows = rt["n_tiles"] * FFN_TILE
    assert E * CAP >= FFN_TILE
    grid_spec = pltpu.PrefetchScalarGridSpec(
        num_scalar_prefetch=4,
        grid=(B, T // tb),
        in_specs=[pl.BlockSpec((1, tb, D), lambda b, t, *_: (b, t, 0)),
                  pl.BlockSpec((1, E, tb), lambda b, t, *_: (b, 0, t)),
                  pl.BlockSpec((tb, tb), lambda b, t, *_: (0, 0))],
        out_specs=pl.BlockSpec(memory_space=pl.ANY),
        scratch_shapes=[pltpu.VMEM((2, E * CAP, D), BF16), pltpu.VMEM((CAP, D), BF16), pltpu.VMEM((E, tb), F32),
                        pltpu.SemaphoreType.DMA((2,))],
    )
    return pl.pallas_call(
        _dispatch_kernel, grid_spec=grid_spec,
        out_shape=jax.ShapeDtypeStruct((rows, D), BF16),
        compiler_params=_params(("arbitrary", "arbitrary")),
        name="dispatch",
    )(rt["start"], rt["cnt"], rt["over"], rt["zrow"], h2, gates_t, tri)


def _ffn_kernel(te_ref, src_ref, valid_ref, x_ref, wgu_ref, bgu_ref, wd_ref, bdn_ref, o_ref):
    i = pl.program_id(0)

    @pl.when(valid_ref[i] > 0)
    def _():
        h = jnp.dot(x_ref[...], wgu_ref[0], preferred_element_type=F32) + bgu_ref[0]
        h_glu = jnp.minimum(h[:, :D_FF], SWIGLU_LIMIT)
        h_lin = jnp.clip(h[:, D_FF:], -SWIGLU_LIMIT, SWIGLU_LIMIT)
        y = (h_lin + 1.0) * (h_glu * _sigmoid(SWIGLU_ALPHA * h_glu))
        o_ref[...] = (_mm(y, wd_ref[0]) + bdn_ref[0]).astype(o_ref.dtype)

    @pl.when(valid_ref[i] == 0)
    def _():
        o_ref[...] = jnp.zeros_like(o_ref)


def _ffn_call(xs, rt, w_gu, b_gu, w_down, b_down):
    rows, D = xs.shape
    E = w_gu.shape[0]
    grid_spec = pltpu.PrefetchScalarGridSpec(
        num_scalar_prefetch=3,
        grid=(rt["n_tiles"],),
        in_specs=[pl.BlockSpec((FFN_TILE, D), lambda i, te, src, valid: (src[i], 0)),
                  pl.BlockSpec((1, D, 2 * D_FF), lambda i, te, src, valid: (te[i], 0, 0)),
                  pl.BlockSpec((1, 1, 2 * D_FF), lambda i, te, src, valid: (te[i], 0, 0)),
                  pl.BlockSpec((1, D_FF, D), lambda i, te, src, valid: (te[i], 0, 0)),
                  pl.BlockSpec((1, 1, D), lambda i, te, src, valid: (te[i], 0, 0))],
        out_specs=pl.BlockSpec((FFN_TILE, D), lambda i, te, src, valid: (i, 0)),
    )
    return pl.pallas_call(
        _ffn_kernel, grid_spec=grid_spec,
        out_shape=jax.ShapeDtypeStruct((rows, D), BF16),
        compiler_params=_params(("arbitrary",)),
        name="ffn",
    )(rt["tile_e"], rt["src"], rt["valid"], xs, w_gu, b_gu.reshape(E, 1, -1), w_down,
      b_down.reshape(E, 1, -1))


def _combine_kernel(start_ref, cnt_ref, over_ref, g_ref, x1_ref, mod_ref, tri_ref, lg_ref, lb_ref, ys_ref, o_ref,
                    ybuf, xbuf, rank_s, acc_s, sem):
    E, tb = g_ref.shape[1], g_ref.shape[2]
    n_blocks = pl.num_programs(0) * pl.num_programs(1)
    blk = pl.program_id(0) * pl.num_programs(1) + pl.program_id(1)
    cur = blk % 2

    def fetch(half, e, block):
        row = start_ref[e * n_blocks + block]
        return pltpu.make_async_copy(ys_ref.at[pl.ds(pl.multiple_of(row, ROW_ALIGN), CAP)],
                                     ybuf.at[half, pl.ds(e * CAP, CAP)], sem.at[half])

    @pl.when(blk == 0)
    def _():
        for e in range(E):
            fetch(0, e, 0).start()

    @pl.when(blk + 1 < n_blocks)
    def _():
        for e in range(E):
            fetch(1 - cur, e, blk + 1).start()

    g = g_ref[0]
    rank = _block_ranks(g, tri_ref[...])
    slot = lax.broadcasted_iota(jnp.int32, (CAP, tb), 0).astype(F32)
    for e in range(E):
        fetch(cur, e, blk).wait()
    n_part = 4
    per = E // n_part
    acc = None
    for q in range(n_part):
        w = jnp.concatenate([jnp.where(rank[e:e + 1] == slot, g[e:e + 1], 0.0).astype(BF16)
                             for e in range(q * per, (q + 1) * per)], axis=0)
        part = _mm_tn(w, ybuf[cur, q * per * CAP:(q + 1) * per * CAP])
        acc = part if acc is None else acc + part
    acc_s[...] = acc

    @pl.when(over_ref[blk] > 0)
    def _():
        rank_s[...] = rank

        def per_expert(e, carry):
            def per_chunk(c, carry):
                row0 = start_ref[e * n_blocks + blk] + c * CAP
                cp = pltpu.make_async_copy(ys_ref.at[pl.ds(pl.multiple_of(row0, ROW_ALIGN), CAP)], xbuf, sem.at[2])
                cp.start()
                row = rank_s[pl.ds(e, 1), :] - (c * CAP).astype(F32)
                w = jnp.where(row == slot, g_ref[0, pl.ds(e, 1), :], 0.0).astype(BF16)
                cp.wait()
                acc_s[...] += _mm_tn(w, xbuf[...])
                return carry
            n_chunks = (cnt_ref[e * n_blocks + blk] + CAP - 1) // CAP
            return lax.fori_loop(1, n_chunks, per_chunk, carry)
        lax.fori_loop(0, E, per_expert, 0)

    m = mod_ref[0]
    o_ref[0] = _ln(DEEPNORM_ALPHA * x1_ref[0] + (1.0 + m[5:6]) * acc_s[...], lg_ref[...], lb_ref[...], LN_EPS)


def _combine_call(ys, gates_t, x1, mod, tri, rt, ln2_g, ln2_b):
    B, T, D = x1.shape
    E = gates_t.shape[1]
    tb = tri.shape[0]
    const = lambda b, t, *_: (0, 0)
    grid_spec = pltpu.PrefetchScalarGridSpec(
        num_scalar_prefetch=3,
        grid=(B, T // tb),
        in_specs=[pl.BlockSpec((1, E, tb), lambda b, t, *_: (b, 0, t)),
                  pl.BlockSpec((1, tb, D), lambda b, t, *_: (b, t, 0)),
                  pl.BlockSpec((1, 6, D), lambda b, t, *_: (b, 0, 0)),
                  pl.BlockSpec((tb, tb), const),
                  pl.BlockSpec((1, D), const), pl.BlockSpec((1, D), const),
                  pl.BlockSpec(memory_space=pl.ANY)],
        out_specs=pl.BlockSpec((1, tb, D), lambda b, t, *_: (b, t, 0)),
        scratch_shapes=[pltpu.VMEM((2, E * CAP, D), BF16), pltpu.VMEM((CAP, D), BF16), pltpu.VMEM((E, tb), F32),
                        pltpu.VMEM((tb, D), F32), pltpu.SemaphoreType.DMA((3,))],
    )
    return pl.pallas_call(
        _combine_kernel, grid_spec=grid_spec,
        out_shape=jax.ShapeDtypeStruct((B, T, D), F32),
        compiler_params=_params(("arbitrary", "arbitrary")),
        name="combine",
    )(rt["start"], rt["cnt"], rt["over"], gates_t, x1, mod, tri, _row(ln2_g), _row(ln2_b), ys)


def _moe_call(h2, gates_t, x1, mod, w_gu, b_gu, w_down, b_down, ln2_g, ln2_b):
    tb = min(TOKEN_BLOCK, h2.shape[1])
    pos = lax.broadcasted_iota(jnp.int32, (tb, tb), 0)
    tri = (pos < pos.T).astype(BF16)
    rt = _routing(gates_t, tb)
    xs = _dispatch_call(h2, gates_t, tri, rt)
    ys = _ffn_call(xs, rt, w_gu, b_gu, w_down, b_down)
    return _combine_call(ys, gates_t, x1, mod, tri, rt, ln2_g, ln2_b)


def _block_diag_cat(w):
    z = jnp.zeros_like(w[0])
    return jnp.concatenate([jnp.concatenate([w[0], z], 1), jnp.concatenate([z, w[1]], 1)], 0)


def _encode(x, mod, W):
    p_r, p_c = _inproj_call(x, mod, W["in_g"], W["in_b"], W["w_in_r"], W["w_in_c"])
    r, v, kk, lw, kd, bb, g, bonus = _pre_call(p_r, W["mu_shift"], W["w0c"], W["w2c"], W["a0c"], W["a2c"],
                                               W["g_up"], W["k_k"], W["k_a"], W["r_k"], W["bd"])
    wkv_f, wkv_b = _scan_call(r, v, kk, lw, kd, bb)
    y_b = _conv_call(p_c, W["dw_w"], W["dw_b"], W["cln_g"], W["cln_b"])
    x1, h2, gates = _post_call(x, mod, W["in_g"], W["in_b"], wkv_f, wkv_b, bonus, g, y_b, W["gn_g"], W["gn_b"],
                               W["bd"], W["w_out"], W["ln1_g"], W["ln1_b"], W["w_router_p"], W["b_router_p"])
    return _moe_call(h2, gates, x1, mod, W["w_gu"], W["b_gu"], W["w_down"], W["b_down"], W["ln2_g"], W["ln2_b"])


def kernel(x_prompt, x_sample, c_prompt, c_sample, in_g, in_b, w_mod, b_mod, w_in, mu_shift, w0, w2, a0, a2, g_up, k_k, k_a, r_k, gn_g, gn_b, dw_w, dw_b, cln_g, cln_b, w_out, ln1_g, ln1_b, w_router, b_router, w_gu, b_gu, w_down, b_down, ln2_g, ln2_b):
    l = 0
    head = lax.broadcasted_iota(jnp.int32, (D_RWKV, D_RWKV), 0) // HEAD_DIM
    W = dict(
        in_g=in_g, in_b=in_b,
        w_in_r=w_in[l][:, :N_SHIFT_COLS].astype(BF16), w_in_c=w_in[l][:, N_SHIFT_COLS:].astype(BF16),
        mu_shift=mu_shift[l],
        w0c=w0[l].reshape(1, -1), w2c=_block_diag_cat(w2[l]).astype(BF16),
        a0c=a0[l].reshape(1, -1), a2c=_block_diag_cat(a2[l]).astype(BF16),
        g_up=g_up[l].astype(BF16), k_k=_row(k_k[l]), k_a=_row(k_a[l]), r_k=_row(r_k[l]),
        bd=(head == head.T).astype(BF16),
        gn_g=gn_g[l], gn_b=gn_b[l], dw_w=dw_w[l], dw_b=dw_b[l], cln_g=cln_g[l], cln_b=cln_b[l],
        w_out=w_out[l].astype(BF16), ln1_g=ln1_g[l], ln1_b=ln1_b[l],
        w_router_p=jnp.pad(w_router[l], ((0, 0), (0, LANES - N_EXPERTS))),
        b_router_p=jnp.pad(b_router[l], (0, LANES - N_EXPERTS), constant_values=-1e30).reshape(1, -1),
        w_gu=w_gu[l].astype(BF16), b_gu=b_gu[l], w_down=w_down[l].astype(BF16), b_down=b_down[l],
        ln2_g=ln2_g[l], ln2_b=ln2_b[l],
    )
    nb = x_prompt.shape[0]
    mod = _mod_call(jnp.concatenate([c_prompt, c_sample], 0), w_mod[l], b_mod[l])
    return (_encode(x_prompt, mod[:nb], W), _encode(x_sample, mod[nb:], W))
```

```python
import math

import jax
import jax.numpy as jnp
from jax import lax
from jax.experimental import pallas as pl
from jax.experimental.pallas import tpu as pltpu

F32 = jnp.float32
BF16 = jnp.bfloat16

D_MODEL = 1024
D_RWKV = 512
HEAD_DIM = 64
N_HEADS = 8
D_CONV = 512
DECAY_RANK = 64
AAA_RANK = 64
GATE_RANK = 128
CONV_WIDTH = 31
N_EXPERTS = 32
TOP_K = 4
D_FF = 1024
SWIGLU_LIMIT = 7.0
SWIGLU_ALPHA = 1.702
LN_EPS = 1e-5
GN_EPS = 64e-5
DEPTH = 1
DEEPNORM_ALPHA = (2.0 * DEPTH) ** 0.25
EXP_NEG_HALF = math.exp(-0.5)
N_SHIFT_COLS = 3 * D_RWKV + 2 * DECAY_RANK + 2 * AAA_RANK + GATE_RANK

LANES = 128
SUBLANES = 8
HALO = 16
CHUNK = 64
SCAN_BLOCK = 8 * CHUNK
TOKEN_BLOCK = 512
CAP = 128
ROW_ALIGN = 16
FFN_TILE = 1024
GROUP = 4
GROUP_W = GROUP * HEAD_DIM
VMEM_LIMIT = 56 * 1024 * 1024


def _params(sem, flags=None):
    return pltpu.CompilerParams(dimension_semantics=sem, vmem_limit_bytes=VMEM_LIMIT, flags=flags)


def _sigmoid(x):
    return 1.0 / (1.0 + jnp.exp(-x))


def _mm(a, b):
    return jnp.dot(a.astype(BF16), b.astype(BF16), preferred_element_type=F32)


def _mm_nt(a, b):
    return lax.dot_general(a.astype(BF16), b.astype(BF16), (((1,), (1,)), ((), ())),
                           preferred_element_type=F32)


def _mm_tn(a, b):
    return lax.dot_general(a.astype(BF16), b.astype(BF16), (((0,), (0,)), ((), ())),
                           preferred_element_type=F32)


def _split(a, n):
    parts = []
    for _ in range(n):
        h = a.astype(BF16)
        parts.append(h)
        a = a - h.astype(F32)
    return parts


def _mm_exact_rhs(a, b_bf16, n):
    acc = None
    for h in _split(a, n):
        t = jnp.dot(h, b_bf16, preferred_element_type=F32)
        acc = t if acc is None else acc + t
    return acc


def _mm_exact_lhs(a_bf16, b, n):
    acc = None
    for h in _split(b, n):
        t = jnp.dot(a_bf16, h, preferred_element_type=F32)
        acc = t if acc is None else acc + t
    return acc


def _mm3(a, b):
    ah, al = _split(a, 2)
    bh, bl = _split(b, 2)
    return (jnp.dot(ah, bh, preferred_element_type=F32) + jnp.dot(al, bh, preferred_element_type=F32)
            + jnp.dot(ah, bl, preferred_element_type=F32))


def _ln(x, g, b, eps):
    mu = jnp.mean(x, -1, keepdims=True)
    xc = x - mu
    var = jnp.mean(xc * xc, -1, keepdims=True)
    return xc * lax.rsqrt(var + eps) * g + b


def _row(a):
    return a.reshape(1, -1)


def _mod_kernel(c_ref, w_ref, b_ref, o_ref):
    c = c_ref[...]
    o_ref[...] = _mm3(c * _sigmoid(c), w_ref[...]) + b_ref[...]


def _mod_call(c, w_mod, b_mod):
    n, d = c.shape
    npad = -(-n // 8) * 8
    cp = jnp.pad(c, ((0, npad - n), (0, 0)))
    tn = 1536
    out = pl.pallas_call(
        _mod_kernel,
        grid=(6 * d // tn,),
        in_specs=[pl.BlockSpec((npad, d), lambda j: (0, 0)),
                  pl.BlockSpec((d, tn), lambda j: (0, j)),
                  pl.BlockSpec((1, tn), lambda j: (0, j))],
        out_specs=pl.BlockSpec((npad, tn), lambda j: (0, j)),
        out_shape=jax.ShapeDtypeStruct((npad, 6 * d), F32),
        compiler_params=_params(("arbitrary",)),
        name="mod",
    )(cp, w_mod, _row(b_mod))
    return out[:n].reshape(n, 6, d)


def _inproj_kernel(x_ref, mod_ref, g_ref, b_ref, wr_ref, wc_ref, pr_ref, pc_ref):
    x0 = _ln(x_ref[0], g_ref[...], b_ref[...], LN_EPS)
    m = mod_ref[0]
    h = (x0 * (1.0 + m[1:2]) + m[0:1]).astype(BF16)
    pr_ref[0] = jnp.dot(h, wr_ref[...], preferred_element_type=F32).astype(pr_ref.dtype)
    pc_ref[0] = jnp.dot(h, wc_ref[...], preferred_element_type=F32).astype(pc_ref.dtype)


def _inproj_call(x, mod, in_g, in_b, w_r, w_c):
    B, T, D = x.shape
    tm = min(512, T)
    const = lambda b, t: (0, 0)
    return pl.pallas_call(
        _inproj_kernel,
        grid=(B, T // tm),
        in_specs=[pl.BlockSpec((1, tm, D), lambda b, t: (b, t, 0)),
                  pl.BlockSpec((1, 6, D), lambda b, t: (b, 0, 0)),
                  pl.BlockSpec((1, D), const), pl.BlockSpec((1, D), const),
                  pl.BlockSpec(w_r.shape, const), pl.BlockSpec(w_c.shape, const)],
        out_specs=[pl.BlockSpec((1, tm, N_SHIFT_COLS), lambda b, t: (b, t, 0)),
                   pl.BlockSpec((1, tm, 2 * D_CONV), lambda b, t: (b, t, 0))],
        out_shape=[jax.ShapeDtypeStruct((B, T, N_SHIFT_COLS), BF16),
                   jax.ShapeDtypeStruct((B, T, 2 * D_CONV), BF16)],
        compiler_params=_params(("parallel", "parallel")),
        name="inproj",
    )(x, mod, _row(in_g), _row(in_b), w_r, w_c)


def _pre_kernel(p_ref, pp_ref, pn_ref, mu_ref, w0_ref, w2_ref, a0_ref, a2_ref, gup_ref, kkw_ref, ka_ref,
                rk_ref, bd_ref, r_o, v_o, kk_o, lw_o, kd_o, bb_o, g_o, bonus_o):
    t = pl.program_id(1)
    nt = pl.num_programs(1)
    p = p_ref[0].astype(F32)
    tm = p.shape[0]
    prev_row = jnp.where(t > 0, pp_ref[0].astype(F32)[HALO - 1:HALO], 0.0)
    next_row = jnp.where(t < nt - 1, pn_ref[0].astype(F32)[0:1], 0.0)
    row = lax.broadcasted_iota(jnp.int32, p.shape, 0)
    prev = jnp.where(row == 0, prev_row, pltpu.roll(p, 1, 0))
    nxt = jnp.where(row == tm - 1, next_row, pltpu.roll(p, tm - 1, 0))
    mu = mu_ref[...]
    ps = p * (1.0 - mu[0:1] - mu[1:2]) + mu[0:1] * prev + mu[1:2] * nxt

    r = ps[:, 0:D_RWKV]
    k = ps[:, D_RWKV:2 * D_RWKV]
    v = ps[:, 2 * D_RWKV:3 * D_RWKV]
    o = 3 * D_RWKV
    wl = ps[:, o:o + 2 * DECAY_RANK]
    al = ps[:, o + 2 * DECAY_RANK:o + 2 * DECAY_RANK + 2 * AAA_RANK]
    gl = ps[:, o + 2 * DECAY_RANK + 2 * AAA_RANK:]

    lw = -EXP_NEG_HALF * _sigmoid(w0_ref[...] + _mm(jnp.tanh(wl), w2_ref[...]))
    a = _sigmoid(a0_ref[...] + _mm(al, a2_ref[...]))
    g = _mm(_sigmoid(gl), gup_ref[...])

    bd = bd_ref[...]
    kk = k * kkw_ref[...]
    ss = _mm_exact_rhs(kk * kk, bd, 1)
    kk = kk * lax.rsqrt(jnp.maximum(ss, 1e-24))
    ka = ka_ref[...]
    a0d, a1d = a[:, :D_RWKV], a[:, D_RWKV:]
    k0 = k * (1.0 + (a0d - 1.0) * ka)
    k1 = k * (1.0 + (a1d - 1.0) * ka)
    bonus = _mm_exact_rhs(r * (k0 + k1) * rk_ref[...], bd, 1) * v

    r_o[0] = r.astype(r_o.dtype)
    v_o[0] = v.astype(v_o.dtype)
    kk_o[0] = kk.astype(kk_o.dtype)
    lw_o[0] = lw
    kd_o[0, :, :D_RWKV] = k0.astype(kd_o.dtype)
    kd_o[0, :, D_RWKV:] = k1.astype(kd_o.dtype)
    bb_o[0, :, :D_RWKV] = (kk * a0d).astype(bb_o.dtype)
    bb_o[0, :, D_RWKV:] = (kk * a1d).astype(bb_o.dtype)
    g_o[0] = g.astype(g_o.dtype)
    bonus_o[0] = bonus.astype(bonus_o.dtype)


def _pre_call(p_r, mu_shift, w0c, w2c, a0c, a2c, g_up, k_k, k_a, r_k, bd):
    B, T, C = p_r.shape
    tm = min(512, T)
    nh = tm // HALO
    const = lambda b, t: (0, 0)
    tile = lambda w: pl.BlockSpec((1, tm, w), lambda b, t: (b, t, 0))
    outs = [D_RWKV, D_RWKV, D_RWKV, 2 * D_RWKV, 2 * D_RWKV, 2 * D_RWKV, D_RWKV, D_RWKV]
    return pl.pallas_call(
        _pre_kernel,
        grid=(B, T // tm),
        in_specs=[tile(C),
                  pl.BlockSpec((1, HALO, C), lambda b, t: (b, jnp.maximum(t * nh - 1, 0), 0)),
                  pl.BlockSpec((1, HALO, C), lambda b, t: (b, jnp.minimum((t + 1) * nh, T // HALO - 1), 0)),
                  pl.BlockSpec(mu_shift.shape, const), pl.BlockSpec(w0c.shape, const),
                  pl.BlockSpec(w2c.shape, const), pl.BlockSpec(a0c.shape, const),
                  pl.BlockSpec(a2c.shape, const), pl.BlockSpec(g_up.shape, const),
                  pl.BlockSpec(k_k.shape, const), pl.BlockSpec(k_a.shape, const),
                  pl.BlockSpec(r_k.shape, const), pl.BlockSpec(bd.shape, const)],
        out_specs=[tile(w) for w in outs],
        out_shape=[jax.ShapeDtypeStruct((B, T, w), F32 if i == 3 else BF16) for i, w in enumerate(outs)],
        compiler_params=_params(("parallel", "parallel")),
        name="pre",
    )(p_r, p_r, p_r, mu_shift, w0c, w2c, a0c, a2c, g_up, k_k, k_a, r_k, bd)


def _chains_chunk(chains):
    assert CHUNK == HEAD_DIM
    L = chains[0]["rt"].shape[0]
    R = GROUP * L
    rowh = lax.broadcasted_iota(jnp.int32, (R, GROUP_W), 0) // L
    colh = lax.broadcasted_iota(jnp.int32, (R, GROUP_W), 1) // HEAD_DIM
    own = rowh == colh

    def bd(x):
        return jnp.where(own, jnp.concatenate([x.astype(BF16)] * GROUP, axis=0), jnp.zeros((), BF16))

    def dot(a, b):
        return jnp.dot(a.astype(BF16), b, preferred_element_type=F32)

    t_row = lax.broadcasted_iota(jnp.int32, (L, GROUP_W), 0)
    i_col = lax.broadcasted_iota(jnp.int32, (L, GROUP_W), 1) % L
    eye = jnp.where(t_row == i_col, 1.0, 0.0)
    masks = {False: (i_col < t_row, i_col <= t_row), True: (i_col > t_row, i_col >= t_row)}

    zr = [jnp.concatenate([c["zt"], c["rt"]], axis=0).astype(BF16) for c in chains]
    a_b = [_mm_nt(x, bd(c["bt"])) for x, c in zip(zr, chains)]
    a_k = [_mm_nt(x, bd(c["kt"])) for x, c in zip(zr, chains)]
    n_zb = [jnp.where(masks[c["reverse"]][0], a[:L], 0.0) for c, a in zip(chains, a_b)]
    a_rb = [jnp.where(masks[c["reverse"]][1], a[L:], 0.0) for c, a in zip(chains, a_b)]
    a_zr = [jnp.concatenate([jnp.where(masks[c["reverse"]][0], a[:L], 0.0),
                             jnp.where(masks[c["reverse"]][1], a[L:], 0.0)], axis=0) for c, a in zip(chains, a_k)]

    tinv = [eye + n for n in n_zb]
    pw = [dot(n, bd(n)) for n in n_zb]
    levels = L.bit_length() - 2
    for j in range(1, levels + 1):
        if j < levels:
            both = [dot(jnp.concatenate([p, t], axis=0), bd(p)) for p, t in zip(pw, tinv)]
            pw = [b[:L] for b in both]
            tinv = [t + b[L:] for t, b in zip(tinv, both)]
        else:
            tinv = [t + dot(t, bd(p)) for p, t in zip(pw, tinv)]

    av = [dot(a, bd(c["v"])) for a, c in zip(a_zr, chains)]

    def rows(x):
        return jnp.concatenate([x[:, h * HEAD_DIM:(h + 1) * HEAD_DIM] for h in range(GROUP)], axis=0)

    def advance(idx, S):
        zs = [_mm_nt(zr[i], bd(s)) for i, s in zip(idx, S)]
        u = [dot(tinv[i], bd(z[:L] + av[i][:L])) for i, z in zip(idx, zs)]
        outs = [z[L:] + av[i][L:] + dot(a_rb[i], bd(x)) for i, z, x in zip(idx, zs, u)]
        nxt = [s * chains[i]["wtot"] + _mm_tn(rows(x), bd(chains[i]["bhat"]))
               + _mm_tn(rows(chains[i]["v"]), bd(chains[i]["khat"])) for i, s, x in zip(idx, S, u)]
        return outs, nxt

    return advance


def _prep_chains(r, v, kk, lw, kd, bb, reverse):
    L = r.shape[0]
    r, kk, kd, bb = (x.astype(F32) for x in (r, kk, kd, bb))
    ti = lax.broadcasted_iota(jnp.int32, (L, L), 0)
    ii = lax.broadcasted_iota(jnp.int32, (L, L), 1)
    tri = jnp.where((ii >= ti) if reverse else (ii <= ti), 1.0, 0.0).astype(BF16)
    cum = _mm_exact_lhs(tri, lw, 3)
    tot = cum[0:1] if reverse else cum[L - 1:L]
    ec, eci, ecx, eto = jnp.exp(cum), jnp.exp(-cum), jnp.exp(cum - lw), jnp.exp(tot - cum)
    full = dict(rt=r * ec, zt=-(kk * ecx), kt=kd * eci, bt=bb * eci, khat=kd * eto, bhat=bb * eto, v=v,
                wtot=jnp.exp(tot))
    chains = []
    for g in range(D_RWKV // GROUP_W):
        c = {k: x[:, g * GROUP_W:(g + 1) * GROUP_W] for k, x in full.items()}
        c["reverse"] = reverse
        chains.append(c)
    return chains


def _scan_kernel(rf, vf, kkf, lwf, kdf, bbf, rb, vb, kkb, lwb, kdb, bbb, of, ob, s_ref):
    @pl.when(pl.program_id(1) == 0)
    def _():
        s_ref[...] = jnp.zeros_like(s_ref)

    n_sub = rf.shape[1] // CHUNK
    n_grp = D_RWKV // GROUP_W
    per = 2 * n_grp
    chains, spans = [], []
    for i in range(n_sub):
        lf = pl.ds(i * CHUNK, CHUNK)
        lb = pl.ds((n_sub - 1 - i) * CHUNK, CHUNK)
        spans.append((lf, lb))
        chains += _prep_chains(rf[0, lf], vf[0, lf], kkf[0, lf], lwf[0, lf], kdf[0, lf], bbf[0, lf], False)
        chains += _prep_chains(rb[0, lb], vb[0, lb], kkb[0, lb], lwb[0, lb], kdb[0, lb], bbb[0, lb], True)
    advance = _chains_chunk(chains)
    sf, sb = s_ref[0], s_ref[1]
    states = ([sf[:, g * GROUP_W:(g + 1) * GROUP_W] for g in range(n_grp)]
              + [sb[:, g * GROUP_W:(g + 1) * GROUP_W] for g in range(n_grp)])
    for i, (lf, lb) in enumerate(spans):
        outs, states = advance(list(range(i * per, (i + 1) * per)), states)
        of[0, lf] = jnp.concatenate(outs[:n_grp], axis=1)
        ob[0, lb] = jnp.concatenate(outs[n_grp:], axis=1)
    s_ref[0] = jnp.concatenate(states[:n_grp], axis=1)
    s_ref[1] = jnp.concatenate(states[n_grp:], axis=1)


def _scan_call(r, v, kk, lw, kd, bb):
    B, T, C = r.shape
    L = min(SCAN_BLOCK, T)
    nc = T // L
    fwd = lambda j: pl.BlockSpec((1, L, C), lambda b, c: (b, c, j))
    bwd = lambda j: pl.BlockSpec((1, L, C), lambda b, c: (b, nc - 1 - c, j))
    return pl.pallas_call(
        _scan_kernel,
        grid=(B, nc),
        in_specs=[fwd(0)] * 6 + [bwd(0)] * 3 + [bwd(1)] * 3,
        out_specs=[fwd(0), bwd(0)],
        out_shape=[jax.ShapeDtypeStruct((B, T, C), F32)] * 2,
        scratch_shapes=[pltpu.VMEM((2, HEAD_DIM, C), F32)],
        compiler_params=_params(("parallel", "arbitrary")),
        name="scan",
    )(r, v, kk, lw, kd, bb, r, v, kk, lw, kd, bb)


def _conv_kernel(p_ref, pp_ref, pn_ref, w_ref, b_ref, g_ref, be_ref, o_ref, h_ref, hs_ref):
    t = pl.program_id(1)
    nt = pl.num_programs(1)
    tm = p_ref.shape[1]

    def glu(x):
        x = x.astype(F32)
        return x[:, :D_CONV] * _sigmoid(x[:, D_CONV:])

    h_ref[0:HALO] = jnp.where(t > 0, glu(pp_ref[0]), 0.0)
    h_ref[HALO:HALO + tm] = glu(p_ref[0])
    h_ref[HALO + tm:] = jnp.where(t < nt - 1, glu(pn_ref[0]), 0.0)
    rows = hs_ref.shape[1]
    for s in range(1, SUBLANES):
        hs_ref[s - 1] = h_ref[s:s + rows]
    w = w_ref[...]
    acc = jnp.zeros((tm, D_CONV), F32) + b_ref[...]
    base = HALO - CONV_WIDTH // 2
    for j in range(CONV_WIDTH):
        off, s = (base + j) // SUBLANES * SUBLANES, (base + j) % SUBLANES
        tap = h_ref[off:off + tm] if s == 0 else hs_ref[s - 1, off:off + tm]
        acc = acc + w[j:j + 1] * tap
    y = _ln(acc, g_ref[...], be_ref[...], LN_EPS)
    o_ref[0] = (y * _sigmoid(y)).astype(o_ref.dtype)


def _conv_call(p_c, dw_w, dw_b, cln_g, cln_b):
    B, T, C = p_c.shape
    tm = min(512, T)
    nh = tm // HALO
    const = lambda b, t: (0, 0)
    return pl.pallas_call(
        _conv_kernel,
        grid=(B, T // tm),
        in_specs=[pl.BlockSpec((1, tm, C), lambda b, t: (b, t, 0)),
                  pl.BlockSpec((1, HALO, C), lambda b, t: (b, jnp.maximum(t * nh - 1, 0), 0)),
                  pl.BlockSpec((1, HALO, C), lambda b, t: (b, jnp.minimum((t + 1) * nh, T // HALO - 1), 0)),
                  pl.BlockSpec(dw_w.shape, const), pl.BlockSpec((1, D_CONV), const),
                  pl.BlockSpec((1, D_CONV), const), pl.BlockSpec((1, D_CONV), const)],
        out_specs=pl.BlockSpec((1, tm, D_CONV), lambda b, t: (b, t, 0)),
        out_shape=jax.ShapeDtypeStruct((B, T, D_CONV), BF16),
        scratch_shapes=[pltpu.VMEM((tm + 2 * HALO, D_CONV), F32),
                        pltpu.VMEM((SUBLANES - 1, tm + 2 * HALO - SUBLANES, D_CONV), F32)],
        compiler_params=_params(("parallel", "parallel")),
        name="conv",
    )(p_c, p_c, p_c, dw_w, _row(dw_b), _row(cln_g), _row(cln_b))


def _post_kernel(x_ref, mod_ref, ing_ref, inb_ref, wf_ref, wb_ref, bonus_ref, g_ref, yb_ref, gng_ref, gnb_ref,
                 bd_ref, wo_ref, l1g_ref, l1b_ref, wr_ref, br_ref, x1_o, h2_o, gate_o):
    m = mod_ref[0]
    x0 = _ln(x_ref[0], ing_ref[...], inb_ref[...], LN_EPS)
    bd = bd_ref[...]
    wkv = wf_ref[0] + wb_ref[0]
    mu = _mm_exact_rhs(wkv, bd, 2) * (1.0 / HEAD_DIM)
    wc = wkv - mu
    var = _mm_exact_rhs(wc * wc, bd, 1) * (1.0 / HEAD_DIM)
    o = wc * lax.rsqrt(var + GN_EPS) * gng_ref[...] + gnb_ref[...]
    ya = (o + bonus_ref[0].astype(F32)) * g_ref[0].astype(F32)
    wo = wo_ref[...]
    mix = _mm(ya, wo[:D_RWKV]) + jnp.dot(yb_ref[0], wo[D_RWKV:], preferred_element_type=F32)
    x1 = _ln(DEEPNORM_ALPHA * x0 + (1.0 + m[2:3]) * mix, l1g_ref[...], l1b_ref[...], LN_EPS)
    x1_o[0] = x1
    h2 = x1 * (1.0 + m[4:5]) + m[3:4]
    h2_o[0] = h2.astype(h2_o.dtype)

    logits = _mm3(h2, wr_ref[...]) + br_ref[...]
    lane = lax.broadcasted_iota(jnp.int32, logits.shape, 1).astype(F32)
    work = logits
    sel = jnp.zeros(logits.shape, F32)
    top = None
    for _ in range(TOP_K):
        mx = jnp.max(work, -1, keepdims=True)
        first = jnp.min(jnp.where(work == mx, lane, float(LANES)), -1, keepdims=True)
        pick = lane == first
        sel = jnp.where(pick, 1.0, sel)
        work = jnp.where(pick, -jnp.inf, work)
        top = mx if top is None else top
    e = jnp.where(sel > 0.0, jnp.exp(logits - top), 0.0)
    gates = e / jnp.sum(e, -1, keepdims=True)
    gate_o[0] = gates.T[:N_EXPERTS]


def _post_call(x, mod, in_g, in_b, wkv_f, wkv_b, bonus, g, y_b, gn_g, gn_b, bd, w_out, ln1_g, ln1_b,
               w_router_p, b_router_p):
    B, T, D = x.shape
    tm = min(512, T)
    const = lambda b, t: (0, 0)
    tile = lambda w: pl.BlockSpec((1, tm, w), lambda b, t: (b, t, 0))
    vec = lambda w: pl.BlockSpec((1, w), const)
    return pl.pallas_call(
        _post_kernel,
        grid=(B, T // tm),
        in_specs=[tile(D), pl.BlockSpec((1, 6, D), lambda b, t: (b, 0, 0)), vec(D), vec(D),
                  tile(D_RWKV), tile(D_RWKV), tile(D_RWKV), tile(D_RWKV), tile(D_CONV),
                  vec(D_RWKV), vec(D_RWKV), pl.BlockSpec(bd.shape, const),
                  pl.BlockSpec(w_out.shape, const), vec(D), vec(D),
                  pl.BlockSpec(w_router_p.shape, const), vec(LANES)],
        out_specs=[tile(D), tile(D), pl.BlockSpec((1, N_EXPERTS, tm), lambda b, t: (b, 0, t))],
        out_shape=[jax.ShapeDtypeStruct((B, T, D), F32), jax.ShapeDtypeStruct((B, T, D), BF16),
                   jax.ShapeDtypeStruct((B, N_EXPERTS, T), F32)],
        compiler_params=_params(("parallel", "parallel")),
        name="post",
    )(x, mod, _row(in_g), _row(in_b), wkv_f, wkv_b, bonus, g, y_b, _row(gn_g), _row(gn_b), bd, w_out,
      _row(ln1_g), _row(ln1_b), w_router_p, b_router_p)


def _routing(gates_t, tb):
    B, E, T = gates_t.shape
    nb = T // tb
    n_blocks = B * nb
    cnt = jnp.sum((gates_t > 0.0).reshape(B, E, nb, tb), -1, dtype=jnp.int32)
    cnt = cnt.transpose(1, 0, 2).reshape(E, n_blocks)
    cnt_al = (cnt + ROW_ALIGN - 1) // ROW_ALIGN * ROW_ALIGN
    n_e = jnp.sum(cnt_al, 1)
    seg = (n_e + CAP + FFN_TILE - 1) // FFN_TILE * FFN_TILE
    seg_end = jnp.cumsum(seg)
    seg_start = seg_end - seg
    start = seg_start[:, None] + jnp.cumsum(cnt_al, 1) - cnt_al
    z0 = seg_start + n_e // FFN_TILE * FFN_TILE
    zrow = jnp.concatenate([z0, jnp.minimum(z0 + FFN_TILE, seg_end[-1] - FFN_TILE), seg_end[-1:] // FFN_TILE])
    rows_max = TOP_K * B * T + (ROW_ALIGN - 1) * n_blocks * E + E * (CAP + FFN_TILE - 1)
    n_tiles = -(-rows_max // FFN_TILE)
    idx = jnp.arange(n_tiles, dtype=jnp.int32)
    n_valid = seg_end[-1] // FFN_TILE
    src = jnp.minimum(idx, n_valid - 1)
    tile_e = jnp.sum(seg_end[None, :] <= (src * FFN_TILE)[:, None], 1, dtype=jnp.int32)
    over = jnp.max(cnt, 0) > CAP
    return dict(start=start.reshape(-1), cnt=cnt.reshape(-1), over=over.astype(jnp.int32), zrow=zrow,
                tile_e=tile_e, src=src, valid=(idx < n_valid).astype(jnp.int32), n_tiles=n_tiles)


def _block_ranks(g, tri):
    routed = g > 0.0
    rank = jnp.dot(jnp.where(routed, 1.0, 0.0).astype(BF16), tri, preferred_element_type=F32)
    return jnp.where(routed, rank, -1.0)


def _group_copy(src, hbm, sem, row):
    return pltpu.make_async_copy(src, hbm.at[pl.ds(pl.multiple_of(row, ROW_ALIGN), CAP)], sem)


def _dispatch_kernel(start_ref, cnt_ref, over_ref, zrow_ref, h_ref, g_ref, tri_ref, xs_out, stage, xbuf, rank_s,
                     sem):
    E, tb = g_ref.shape[1], g_ref.shape[2]
    n_blocks = pl.num_programs(0) * pl.num_programs(1)
    blk = pl.program_id(0) * pl.num_programs(1) + pl.program_id(1)
    cur = blk % 2

    @pl.when(blk == 0)
    def _():
        zeros = stage.at[1, pl.ds(0, FFN_TILE)]

        def zero_copy(row):
            return pltpu.make_async_copy(zeros, xs_out.at[pl.ds(pl.multiple_of(row, FFN_TILE), FFN_TILE)], sem.at[1])

        stage[1, 0:FFN_TILE] = jnp.zeros((FFN_TILE, stage.shape[2]), stage.dtype)
        for i in range(E):
            zero_copy(zrow_ref[i]).start()
        for i in range(E):
            zero_copy(zrow_ref[i]).wait()
        for i in range(E, 2 * E):
            cp = zero_copy(zrow_ref[i])
            cp.start()
            cp.wait()
        first, last = zrow_ref[2 * E], xs_out.shape[0] // FFN_TILE
        lax.fori_loop(first, last, lambda i, c: (zero_copy(i * FFN_TILE).start(), c)[1], 0)
        lax.fori_loop(first, last, lambda i, c: (zero_copy(i * FFN_TILE).wait(), c)[1], 0)

    def group_copies(half, block):
        return [_group_copy(stage.at[half, pl.ds(e * CAP, CAP)], xs_out, sem.at[0], start_ref[e * n_blocks + block])
                for e in range(E)]

    rank = _block_ranks(g_ref[0], tri_ref[...])
    rank_s[...] = rank
    slot = lax.broadcasted_iota(jnp.int32, (CAP, tb), 0).astype(F32)
    h = h_ref[0]
    n_part = 4
    per = E // n_part
    for q in range(n_part):
        sel = jnp.concatenate([jnp.where(rank[e:e + 1] == slot, 1.0, 0.0).astype(BF16)
                               for e in range(q * per, (q + 1) * per)], axis=0)
        stage[cur, q * per * CAP:(q + 1) * per * CAP] = jnp.dot(sel, h, preferred_element_type=F32).astype(BF16)

    @pl.when(blk > 0)
    def _():
        for cp in group_copies(1 - cur, blk - 1):
            cp.wait()

    for cp in group_copies(cur, blk):
        cp.start()

    @pl.when(over_ref[blk] > 0)
    def _():
        def per_expert(e, carry):
            def per_chunk(c, carry):
                row = rank_s[pl.ds(e, 1), :] - (c * CAP).astype(F32)
                sel = jnp.where(row == slot, 1.0, 0.0).astype(BF16)
                xbuf[...] = jnp.dot(sel, h, preferred_element_type=F32).astype(BF16)
                cp = _group_copy(xbuf, xs_out, sem.at[1], start_ref[e * n_blocks + blk] + c * CAP)
                cp.start()
                cp.wait()
                return carry
            n_chunks = (cnt_ref[e * n_blocks + blk] + CAP - 1) // CAP
            return lax.fori_loop(1, n_chunks, per_chunk, carry)
        lax.fori_loop(0, E, per_expert, 0)

    @pl.when(blk == n_blocks - 1)
    def _():
        for cp in group_copies(cur, blk):
            cp.wait()


def _dispatch_call(h2, gates_t, tri, rt):
    B, T, D = h2.shape
    E = gates_t.shape[1]
    tb = tri.shape[0]
    rows = rt["n_tiles"] * FFN_TILE
    assert E * CAP >= FFN_TILE
    grid_spec = pltpu.PrefetchScalarGridSpec(
        num_scalar_prefetch=4,
        grid=(B, T // tb),
        in_specs=[pl.BlockSpec((1, tb, D), lambda b, t, *_: (b, t, 0)),
                  pl.BlockSpec((1, E, tb), lambda b, t, *_: (b, 0, t)),
                  pl.BlockSpec((tb, tb), lambda b, t, *_: (0, 0))],
        out_specs=pl.BlockSpec(memory_space=pl.ANY),
        scratch_shapes=[pltpu.VMEM((2, E * CAP, D), BF16), pltpu.VMEM((CAP, D), BF16), pltpu.VMEM((E, tb), F32),
                        pltpu.SemaphoreType.DMA((2,))],
    )
    return pl.pallas_call(
        _dispatch_kernel, grid_spec=grid_spec,
        out_shape=jax.ShapeDtypeStruct((rows, D), BF16),
        compiler_params=_params(("arbitrary", "arbitrary")),
        name="dispatch",
    )(rt["start"], rt["cnt"], rt["over"], rt["zrow"], h2, gates_t, tri)


def _ffn_kernel(te_ref, src_ref, valid_ref, x_ref, wgu_ref, bgu_ref, wd_ref, bdn_ref, o_ref):
    i = pl.program_id(0)

    @pl.when(valid_ref[i] > 0)
    def _():
        h = jnp.dot(x_ref[...], wgu_ref[0], preferred_element_type=F32) + bgu_ref[0]
        h_glu = jnp.minimum(h[:, :D_FF], SWIGLU_LIMIT)
        h_lin = jnp.clip(h[:, D_FF:], -SWIGLU_LIMIT, SWIGLU_LIMIT)
        y = (h_lin + 1.0) * (h_glu * _sigmoid(SWIGLU_ALPHA * h_glu))
        o_ref[...] = (_mm(y, wd_ref[0]) + bdn_ref[0]).astype(o_ref.dtype)

    @pl.when(valid_ref[i] == 0)
    def _():
        o_ref[...] = jnp.zeros_like(o_ref)


def _ffn_call(xs, rt, w_gu, b_gu, w_down, b_down):
    rows, D = xs.shape
    E = w_gu.shape[0]
    grid_spec = pltpu.PrefetchScalarGridSpec(
        num_scalar_prefetch=3,
        grid=(rt["n_tiles"],),
        in_specs=[pl.BlockSpec((FFN_TILE, D), lambda i, te, src, valid: (src[i], 0)),
                  pl.BlockSpec((1, D, 2 * D_FF), lambda i, te, src, valid: (te[i], 0, 0)),
                  pl.BlockSpec((1, 1, 2 * D_FF), lambda i, te, src, valid: (te[i], 0, 0)),
                  pl.BlockSpec((1, D_FF, D), lambda i, te, src, valid: (te[i], 0, 0)),
                  pl.BlockSpec((1, 1, D), lambda i, te, src, valid: (te[i], 0, 0))],
        out_specs=pl.BlockSpec((FFN_TILE, D), lambda i, te, src, valid: (i, 0)),
    )
    return pl.pallas_call(
        _ffn_kernel, grid_spec=grid_spec,
        out_shape=jax.ShapeDtypeStruct((rows, D), BF16),
        compiler_params=_params(("arbitrary",)),
        name="ffn",
    )(rt["tile_e"], rt["src"], rt["valid"], xs, w_gu, b_gu.reshape(E, 1, -1), w_down,
      b_down.reshape(E, 1, -1))


def _combine_kernel(start_ref, cnt_ref, over_ref, g_ref, x1_ref, mod_ref, tri_ref, lg_ref, lb_ref, ys_ref, o_ref,
                    ybuf, xbuf, rank_s, acc_s, sem):
    E, tb = g_ref.shape[1], g_ref.shape[2]
    n_blocks = pl.num_programs(0) * pl.num_programs(1)
    blk = pl.program_id(0) * pl.num_programs(1) + pl.program_id(1)
    cur = blk % 2

    def fetch(half, e, block):
        row = start_ref[e * n_blocks + block]
        return pltpu.make_async_copy(ys_ref.at[pl.ds(pl.multiple_of(row, ROW_ALIGN), CAP)],
                                     ybuf.at[half, pl.ds(e * CAP, CAP)], sem.at[half])

    @pl.when(blk == 0)
    def _():
        for e in range(E):
            fetch(0, e, 0).start()

    @pl.when(blk + 1 < n_blocks)
    def _():
        for e in range(E):
            fetch(1 - cur, e, blk + 1).start()

    g = g_ref[0]
    rank = _block_ranks(g, tri_ref[...])
    slot = lax.broadcasted_iota(jnp.int32, (CAP, tb), 0).astype(F32)
    for e in range(E):
        fetch(cur, e, blk).wait()
    n_part = 4
    per = E // n_part
    acc = None
    for q in range(n_part):
        w = jnp.concatenate([jnp.where(rank[e:e + 1] == slot, g[e:e + 1], 0.0).astype(BF16)
                             for e in range(q * per, (q + 1) * per)], axis=0)
        part = _mm_tn(w, ybuf[cur, q * per * CAP:(q + 1) * per * CAP])
        acc = part if acc is None else acc + part
    acc_s[...] = acc

    @pl.when(over_ref[blk] > 0)
    def _():
        rank_s[...] = rank

        def per_expert(e, carry):
            def per_chunk(c, carry):
                row0 = start_ref[e * n_blocks + blk] + c * CAP
                cp = pltpu.make_async_copy(ys_ref.at[pl.ds(pl.multiple_of(row0, ROW_ALIGN), CAP)], xbuf, sem.at[2])
                cp.start()
                row = rank_s[pl.ds(e, 1), :] - (c * CAP).astype(F32)
                w = jnp.where(row == slot, g_ref[0, pl.ds(e, 1), :], 0.0).astype(BF16)
                cp.wait()
                acc_s[...] += _mm_tn(w, xbuf[...])
                return carry
            n_chunks = (cnt_ref[e * n_blocks + blk] + CAP - 1) // CAP
            return lax.fori_loop(1, n_chunks, per_chunk, carry)
        lax.fori_loop(0, E, per_expert, 0)

    m = mod_ref[0]
    o_ref[0] = _ln(DEEPNORM_ALPHA * x1_ref[0] + (1.0 + m[5:6]) * acc_s[...], lg_ref[...], lb_ref[...], LN_EPS)


def _combine_call(ys, gates_t, x1, mod, tri, rt, ln2_g, ln2_b):
    B, T, D = x1.shape
    E = gates_t.shape[1]
    tb = tri.shape[0]
    const = lambda b, t, *_: (0, 0)
    grid_spec = pltpu.PrefetchScalarGridSpec(
        num_scalar_prefetch=3,
        grid=(B, T // tb),
        in_specs=[pl.BlockSpec((1, E, tb), lambda b, t, *_: (b, 0, t)),
                  pl.BlockSpec((1, tb, D), lambda b, t, *_: (b, t, 0)),
                  pl.BlockSpec((1, 6, D), lambda b, t, *_: (b, 0, 0)),
                  pl.BlockSpec((tb, tb), const),
                  pl.BlockSpec((1, D), const), pl.BlockSpec((1, D), const),
                  pl.BlockSpec(memory_space=pl.ANY)],
        out_specs=pl.BlockSpec((1, tb, D), lambda b, t, *_: (b, t, 0)),
        scratch_shapes=[pltpu.VMEM((2, E * CAP, D), BF16), pltpu.VMEM((CAP, D), BF16), pltpu.VMEM((E, tb), F32),
                        pltpu.VMEM((tb, D), F32), pltpu.SemaphoreType.DMA((3,))],
    )
    return pl.pallas_call(
        _combine_kernel, grid_spec=grid_spec,
        out_shape=jax.ShapeDtypeStruct((B, T, D), F32),
        compiler_params=_params(("arbitrary", "arbitrary")),
        name="combine",
    )(rt["start"], rt["cnt"], rt["over"], gates_t, x1, mod, tri, _row(ln2_g), _row(ln2_b), ys)


def _moe_call(h2, gates_t, x1, mod, w_gu, b_gu, w_down, b_down, ln2_g, ln2_b):
    tb = min(TOKEN_BLOCK, h2.shape[1])
    pos = lax.broadcasted_iota(jnp.int32, (tb, tb), 0)
    tri = (pos < pos.T).astype(BF16)
    rt = _routing(gates_t, tb)
    xs = _dispatch_call(h2, gates_t, tri, rt)
    ys = _ffn_call(xs, rt, w_gu, b_gu, w_down, b_down)
    return _combine_call(ys, gates_t, x1, mod, tri, rt, ln2_g, ln2_b)


def _block_diag_cat(w):
    z = jnp.zeros_like(w[0])
    return jnp.concatenate([jnp.concatenate([w[0], z], 1), jnp.concatenate([z, w[1]], 1)], 0)


def _encode(x, mod, W):
    p_r, p_c = _inproj_call(x, mod, W["in_g"], W["in_b"], W["w_in_r"], W["w_in_c"])
    r, v, kk, lw, kd, bb, g, bonus = _pre_call(p_r, W["mu_shift"], W["w0c"], W["w2c"], W["a0c"], W["a2c"],
                                               W["g_up"], W["k_k"], W["k_a"], W["r_k"], W["bd"])
    wkv_f, wkv_b = _scan_call(r, v, kk, lw, kd, bb)
    y_b = _conv_call(p_c, W["dw_w"], W["dw_b"], W["cln_g"], W["cln_b"])
    x1, h2, gates = _post_call(x, mod, W["in_g"], W["in_b"], wkv_f, wkv_b, bonus, g, y_b, W["gn_g"], W["gn_b"],
                               W["bd"], W["w_out"], W["ln1_g"], W["ln1_b"], W["w_router_p"], W["b_router_p"])
    return _moe_call(h2, gates, x1, mod, W["w_gu"], W["b_gu"], W["w_down"], W["b_down"], W["ln2_g"], W["ln2_b"])


def kernel(x_prompt, x_sample, c_prompt, c_sample, in_g, in_b, w_mod, b_mod, w_in, mu_shift, w0, w2, a0, a2, g_up, k_k, k_a, r_k, gn_g, gn_b, dw_w, dw_b, cln_g, cln_b, w_out, ln1_g, ln1_b, w_router, b_router, w_gu, b_gu, w_down, b_down, ln2_g, ln2_b):
    l = 0
    head = lax.broadcasted_iota(jnp.int32, (D_RWKV, D_RWKV), 0) // HEAD_DIM
    W = dict(
        in_g=in_g, in_b=in_b,
        w_in_r=w_in[l][:, :N_SHIFT_COLS].astype(BF16), w_in_c=w_in[l][:, N_SHIFT_COLS:].astype(BF16),
        mu_shift=mu_shift[l],
        w0c=w0[l].reshape(1, -1), w2c=_block_diag_cat(w2[l]).astype(BF16),
        a0c=a0[l].reshape(1, -1), a2c=_block_diag_cat(a2[l]).astype(BF16),
        g_up=g_up[l].astype(BF16), k_k=_row(k_k[l]), k_a=_row(k_a[l]), r_k=_row(r_k[l]),
        bd=(head == head.T).astype(BF16),
        gn_g=gn_g[l], gn_b=gn_b[l], dw_w=dw_w[l], dw_b=dw_b[l], cln_g=cln_g[l], cln_b=cln_b[l],
        w_out=w_out[l].astype(BF16), ln1_g=ln1_g[l], ln1_b=ln1_b[l],
        w_router_p=jnp.pad(w_router[l], ((0, 0), (0, LANES - N_EXPERTS))),
        b_router_p=jnp.pad(b_router[l], (0, LANES - N_EXPERTS), constant_values=-1e30).reshape(1, -1),
        w_gu=w_gu[l].astype(BF16), b_gu=b_gu[l], w_down=w_down[l].astype(BF16), b_down=b_down[l],
        ln2_g=ln2_g[l], ln2_b=ln2_b[l],
    )
    nb = x_prompt.shape[0]
    mod = _mod_call(jnp.concatenate([c_prompt, c_sample], 0), w_mod[l], b_mod[l])
    return (_encode(x_prompt, mod[:nb], W), _encode(x_sample, mod[nb:], W))
```
